```python
import math
import jax, jax.numpy as jnp
from jax import lax
import numpy as np

D_MODEL = 2048
BATCH = 2
SEQ = 4096
DEPTH = 2
DEC_BATCH = 32
DEC_SEQ = 64
PAST_LEN = 4096

CHUNK = 64
N_META = 16
MIX_WIDTH = D_MODEL
SB_WIDTH = MIX_WIDTH // 2
SB_HEAD_DIM = 128
N_SB_HEADS = SB_WIDTH // SB_HEAD_DIM
POOL_WIDTH = MIX_WIDTH - SB_WIDTH
POOL_WINDOWS = (2, 4, 8, 16)
N_POOL_GROUPS = len(POOL_WINDOWS)
POOL_GROUP_DIM = POOL_WIDTH // N_POOL_GROUPS
POOL_HIST = max(POOL_WINDOWS) - 1
IN_WIDTH = 3 * SB_WIDTH + POOL_WIDTH
D_FF = ((8 * D_MODEL // 3 + 255) // 256) * 256
BLOCK_Q = 128
RMS_EPS = 1e-6

kernel_name = "hybrid_stickbreak_pool_stream_step"


def rms_norm(x, g):
    xf = x.astype(jnp.float32)
    y = xf * lax.rsqrt(jnp.mean(xf * xf, axis=-1, keepdims=True) + RMS_EPS) * g.astype(jnp.float32)
    return y.astype(x.dtype)


def swiglu(x, wg, wu, wd):
    return (jax.nn.silu(x @ wg) * (x @ wu)) @ wd


def sb_block(q, k, v, q_pos, k_pos):
    z = jnp.einsum('bqhd,bkhd->bhqk', q, k, preferred_element_type=jnp.float32) * (SB_HEAD_DIM ** -0.5)
    mask = (k_pos[None, :] < q_pos[:, None])[None, None]
    log_stay = jnp.where(mask, jax.nn.log_sigmoid(-z), 0.0)
    after = lax.cumsum(log_stay, axis=3, reverse=True) - log_stay
    a = jnp.where(mask, jnp.exp(jax.nn.log_sigmoid(z) + after), 0.0)
    return jnp.einsum('bhqk,bkhd->bqhd', a.astype(v.dtype), v)


def sb_attention(q, k, v, q_pos0):
    B, Tq, H, Dh = q.shape
    k_pos = jnp.arange(k.shape[1])
    if Tq <= BLOCK_Q:
        return sb_block(q, k, v, q_pos0 + jnp.arange(Tq), k_pos)
    nb = -(-Tq // BLOCK_Q)
    pad = nb * BLOCK_Q - Tq
    qp = jnp.pad(q, ((0, 0), (0, pad), (0, 0), (0, 0)))
    qb = qp.reshape(B, nb, BLOCK_Q, H, Dh).transpose(1, 0, 2, 3, 4)
    pb = (q_pos0 + jnp.arange(nb * BLOCK_Q)).reshape(nb, BLOCK_Q)
    ob = lax.map(lambda a: sb_block(a[0], k, v, a[1], k_pos), (qb, pb))
    return ob.transpose(1, 0, 2, 3, 4).reshape(B, nb * BLOCK_Q, H, Dh)[:, :Tq]


def pool_mix(u, hist, n_hist_valid, w_pool, pool_scale):
    B, L, C = u.shape
    P = POOL_HIST
    ext = jnp.concatenate([hist, u], axis=1)
    extf = ext.astype(jnp.float32)
    cs = jnp.concatenate([jnp.zeros((B, 1, C), jnp.float32), jnp.cumsum(extf, axis=1)], axis=1)
    cur = extf[:, P:]
    n_before = (n_hist_valid + jnp.arange(L)).astype(jnp.float32)
    outs = []
    for g, w in enumerate(POOL_WINDOWS):
        sl = slice(g * POOL_GROUP_DIM, (g + 1) * POOL_GROUP_DIM)
        s = cs[:, P + 1:P + 1 + L, sl] - cs[:, P + 1 - w:P + 1 - w + L, sl]
        cnt = jnp.minimum(float(w), n_before + 1.0)[None, :, None]
        outs.append(s / cnt - cur[:, :, sl])
    d = jnp.stack(outs, axis=2).astype(u.dtype)
    y = jnp.einsum('blgc,gcd->blgd', d, w_pool).reshape(B, L, C) * pool_scale
    return y, ext[:, -P:]


def token_mixer(hn, w_in, w_out, w_pool, pool_scale, k_past, v_past, hist, n_hist_valid):
    B, L, _ = hn.shape
    proj = hn @ w_in
    q, k, v, u = jnp.split(proj, [SB_WIDTH, 2 * SB_WIDTH, 3 * SB_WIDTH], axis=-1)
    q = q.reshape(B, L, N_SB_HEADS, SB_HEAD_DIM)
    k = k.reshape(B, L, N_SB_HEADS, SB_HEAD_DIM)
    v = v.reshape(B, L, N_SB_HEADS, SB_HEAD_DIM)
    if k_past is None:
        k_all, v_all, q_pos0 = k, v, 0
    else:
        k_all = jnp.concatenate([k_past.astype(k.dtype), k], axis=1)
        v_all = jnp.concatenate([v_past.astype(v.dtype), v], axis=1)
        q_pos0 = k_past.shape[1]
    o_sb = sb_attention(q, k_all, v_all, q_pos0).reshape(B, L, SB_WIDTH)
    o_pool, new_hist = pool_mix(u, hist.astype(u.dtype), n_hist_valid, w_pool, pool_scale)
    out = jnp.concatenate([o_sb, o_pool], axis=-1) @ w_out
    return out, k, v, new_hist


def trunk_layer(x, w_in, w_out, w_pool, pool_scale, gains,
                f1g, f1u, f1d, f2g, f2u, f2d, k_past, v_past, hist, n_hist_valid):
    h = x + 0.5 * rms_norm(swiglu(rms_norm(x, gains[0]), f1g, f1u, f1d), gains[1])
    m, k, v, new_hist = token_mixer(rms_norm(h, gains[2]), w_in, w_out, w_pool, pool_scale,
                                    k_past, v_past, hist, n_hist_valid)
    h = h + rms_norm(m, gains[3])
    h = h + 0.5 * rms_norm(swiglu(rms_norm(h, gains[4]), f2g, f2u, f2d), gains[5])
    return h, k, v, new_hist


def setup_inputs(seed: int = 0) -> dict:
    key = jax.random.key(seed)
    ks = jax.random.split(key, 20)
    f32 = jnp.float32
    nrm = lambda k, s, sc: jax.random.normal(k, s, f32) * sc
    return {
        "x_prompt": nrm(ks[0], (BATCH, SEQ, D_MODEL), 1.0),
        "x_sample": nrm(ks[1], (DEC_BATCH, DEC_SEQ, D_MODEL), 1.0),
        "cache_k": nrm(ks[2], (DEPTH, DEC_BATCH, PAST_LEN, N_SB_HEADS, SB_HEAD_DIM), 1.0),
        "cache_v": nrm(ks[3], (DEPTH, DEC_BATCH, PAST_LEN, N_SB_HEADS, SB_HEAD_DIM), 1.0),
        "state_pool": nrm(ks[4], (DEPTH, DEC_BATCH, POOL_HIST, POOL_WIDTH), 1.0),
        "meta_tokens": nrm(ks[5], (N_META, D_MODEL), 1.0),
        "w_in": nrm(ks[6], (DEPTH, D_MODEL, IN_WIDTH), D_MODEL ** -0.5),
        "w_out": nrm(ks[7], (DEPTH, MIX_WIDTH, D_MODEL), MIX_WIDTH ** -0.5),
        "w_pool": nrm(ks[8], (DEPTH, N_POOL_GROUPS, POOL_GROUP_DIM, POOL_GROUP_DIM), POOL_GROUP_DIM ** -0.5),
        "pool_scale": 1.0 + nrm(ks[9], (DEPTH, POOL_WIDTH), 0.1),
        "norm_gains": 1.0 + nrm(ks[10], (DEPTH, 6, D_MODEL), 0.1),
        "ffn1_gate": nrm(ks[11], (DEPTH, D_MODEL, D_FF), D_MODEL ** -0.5),
        "ffn1_up": nrm(ks[12], (DEPTH, D_MODEL, D_FF), D_MODEL ** -0.5),
        "ffn1_down": nrm(ks[13], (DEPTH, D_FF, D_MODEL), D_FF ** -0.5),
        "ffn2_gate": nrm(ks[14], (DEPTH, D_MODEL, D_FF), D_MODEL ** -0.5),
        "ffn2_up": nrm(ks[15], (DEPTH, D_MODEL, D_FF), D_MODEL ** -0.5),
        "ffn2_down": nrm(ks[16], (DEPTH, D_FF, D_MODEL), D_FF ** -0.5),
    }


def reference(x_prompt, x_sample, cache_k, cache_v, state_pool, meta_tokens,
              w_in, w_out, w_pool, pool_scale, norm_gains,
              ffn1_gate, ffn1_up, ffn1_down, ffn2_gate, ffn2_up, ffn2_down):
    B = x_prompt.shape[0]
    meta = jnp.broadcast_to(meta_tokens.astype(x_prompt.dtype)[None], (B, N_META, D_MODEL))
    xp = jnp.concatenate([meta, x_prompt], axis=1)
    xs = x_sample
    zero_hist = jnp.zeros((B, POOL_HIST, POOL_WIDTH), x_prompt.dtype)
    kp_l, vp_l, hp_l, ks_l, vs_l, hs_l = [], [], [], [], [], []
    for l in range(DEPTH):
        w = (w_in[l], w_out[l], w_pool[l], pool_scale[l], norm_gains[l],
             ffn1_gate[l], ffn1_up[l], ffn1_down[l], ffn2_gate[l], ffn2_up[l], ffn2_down[l])
        xp, kp, vp, hp = trunk_layer(xp, *w, None, None, zero_hist, 0)
        xs, k_s, v_s, h_s = trunk_layer(xs, *w, cache_k[l], cache_v[l], state_pool[l], POOL_HIST)
        kp_l.append(kp); vp_l.append(vp); hp_l.append(hp)
        ks_l.append(k_s); vs_l.append(v_s); hs_l.append(h_s)
    y_prompt = xp[:, N_META:]
    return (y_prompt, xs,
            jnp.stack(kp_l), jnp.stack(vp_l), jnp.stack(hp_l),
            jnp.stack(ks_l), jnp.stack(vs_l), jnp.stack(hs_l))
```

```python
import functools

import jax
import jax.numpy as jnp
from jax import lax
from jax.experimental import pallas as pl
from jax.experimental.pallas import tpu as pltpu

F32 = jnp.float32
BF16 = jnp.bfloat16

RMS_EPS = 1e-6
HEAD_DIM = 128
N_META = 16
POOL_WINDOWS = (2, 4, 8, 16)
POOL_HIST = max(POOL_WINDOWS) - 1
HALO = 16
SEQ_TILE = 64
ROW_BLOCK = 128
V7X_VMEM_LIMIT = 56 * 1024 * 1024


def _rms(x, g):
    ms = jnp.mean(x * x, axis=-1, keepdims=True)
    return x * lax.rsqrt(ms + RMS_EPS) * g


def _dot(a, b):
    return jnp.dot(a, b, preferred_element_type=F32)


def _ffn_kernel(x_ref, gi_ref, go_ref, wg_ref, wu_ref, wd_ref, o_ref, xn_ref, *, n_f):
    j = pl.program_id(1)

    @pl.when(j == 0)
    def _():
        xn_ref[...] = _rms(x_ref[...], gi_ref[...]).astype(BF16)

    xn = xn_ref[...]
    g = _dot(xn, wg_ref[...])
    u = _dot(xn, wu_ref[...])
    hidden = (g * jax.nn.sigmoid(g) * u).astype(BF16)
    part = _dot(hidden, wd_ref[...])

    @pl.when(j == 0)
    def _():
        o_ref[...] = part

    @pl.when(j > 0)
    def _():
        o_ref[...] += part

    @pl.when(j == n_f - 1)
    def _():
        o_ref[...] = x_ref[...] + 0.5 * _rms(o_ref[...], go_ref[...])


def _ffn(x, gi, go, wg, wu, wd, *, tm, tf):
    m, d = x.shape
    f = wg.shape[1]
    n_f = f // tf
    return pl.pallas_call(
        functools.partial(_ffn_kernel, n_f=n_f),
        grid=(m // tm, n_f),
        in_specs=[
            pl.BlockSpec((tm, d), lambda i, j: (i, 0)),
            pl.BlockSpec((1, d), lambda i, j: (0, 0)),
            pl.BlockSpec((1, d), lambda i, j: (0, 0)),
            pl.BlockSpec((d, tf), lambda i, j: (0, j)),
            pl.BlockSpec((d, tf), lambda i, j: (0, j)),
            pl.BlockSpec((tf, d), lambda i, j: (j, 0)),
        ],
        out_specs=pl.BlockSpec((tm, d), lambda i, j: (i, 0)),
        out_shape=jax.ShapeDtypeStruct((m, d), F32),
        scratch_shapes=[pltpu.VMEM((tm, d), BF16)],
        compiler_params=pltpu.CompilerParams(
            dimension_semantics=("arbitrary", "arbitrary"), vmem_limit_bytes=V7X_VMEM_LIMIT),
        name="ffn",
    )(x, gi, go, wg, wu, wd)


def _inproj_kernel(x_ref, g_ref, w_ref, q_ref, kb_ref, vb_ref, k_ref, v_ref, u_ref, xn_ref, *, q_scale):
    j = pl.program_id(1)

    @pl.when(j == 0)
    def _():
        xn_ref[...] = _rms(x_ref[...], g_ref[...]).astype(BF16)

    proj = _dot(xn_ref[...], w_ref[...])

    @pl.when(j == 0)
    def _():
        q_ref[...] = (proj * q_scale).astype(BF16)

    @pl.when(j == 1)
    def _():
        k_ref[...] = proj
        kb_ref[...] = proj.astype(BF16)

    @pl.when(j == 2)
    def _():
        v_ref[...] = proj
        vb_ref[...] = proj.astype(BF16)

    @pl.when(j == 3)
    def _():
        u_ref[...] = proj


def _inproj(x, g, w_in, *, tm, width):
    m, d = x.shape
    row_blk = lambda i, j: (i, 0)
    out_bf = jax.ShapeDtypeStruct((m, width), BF16)
    out_f = jax.ShapeDtypeStruct((m, width), F32)
    return pl.pallas_call(
        functools.partial(_inproj_kernel, q_scale=HEAD_DIM ** -0.5),
        grid=(m // tm, 4),
        in_specs=[
            pl.BlockSpec((tm, d), row_blk),
            pl.BlockSpec((1, d), lambda i, j: (0, 0)),
            pl.BlockSpec((d, width), lambda i, j: (0, j)),
        ],
        out_specs=[pl.BlockSpec((tm, width), row_blk)] * 6,
        out_shape=[out_bf, out_bf, out_bf, out_f, out_f, out_f],
        scratch_shapes=[pltpu.VMEM((tm, d), BF16)],
        compiler_params=pltpu.CompilerParams(
            dimension_semantics=("arbitrary", "arbitrary"), vmem_limit_bytes=V7X_VMEM_LIMIT),
        name="inproj",
    )(x, g, w_in)


def _sb_block(q, kb, vb, carry, acc, tri, mask):
    z = lax.dot_general(q, kb, (((1,), (1,)), ((), ())), preferred_element_type=F32)
    sp = jnp.maximum(z, 0.0) + jnp.log1p(jnp.exp(-jnp.abs(z)))
    log_stay = -sp
    if mask is not None:
        log_stay = jnp.where(mask, log_stay, 0.0)
    hi = log_stay.astype(BF16)
    lo = (log_stay - hi.astype(F32)).astype(BF16)
    after = _dot(hi, tri) + _dot(lo, tri)
    a = jnp.exp((z - sp) + after + carry)
    if mask is not None:
        a = jnp.where(mask, a, 0.0)
    acc = acc + _dot(a.astype(BF16), vb)
    carry = carry + jnp.sum(log_stay, axis=1, keepdims=True)
    return carry, acc


def _attn_prompt_kernel(q_ref, k_ref, v_ref, mk_ref, mv_ref, tri_ref, o_ref, *, tq, tk):
    i = pl.program_id(2)
    q = q_ref[...]
    tri = tri_ref[...]
    row = lax.broadcasted_iota(jnp.int32, (tq, tk), 0) + i * tq
    col = lax.broadcasted_iota(jnp.int32, (tq, tk), 1)
    c_diag = (i * tq) // tk

    def load(c):
        off = pl.multiple_of(c * tk, tk)
        return k_ref[pl.ds(off, tk), :], v_ref[pl.ds(off, tk), :]

    carry = jnp.zeros((tq, 1), F32)
    acc = jnp.zeros((tq, HEAD_DIM), F32)
    kb, vb = load(c_diag)
    carry, acc = _sb_block(q, kb, vb, carry, acc, tri, col + c_diag * tk < row)

    def body(s, state):
        kb, vb = load(c_diag - 1 - s)
        return _sb_block(q, kb, vb, state[0], state[1], tri, None)

    carry, acc = lax.fori_loop(0, c_diag, body, (carry, acc))

    n_mk = mk_ref.shape[0]
    mcol = lax.broadcasted_iota(jnp.int32, (tq, n_mk), 1)
    carry, acc = _sb_block(q, mk_ref[...], mv_ref[...], carry, acc, tri[:n_mk, :n_mk], mcol < N_META)
    o_ref[...] = acc.astype(o_ref.dtype)


def _attn_prompt(q, kb, vb, tri, *, n_batch, seq, n_heads, meta_blk, tq, tk):
    n_q = seq // tq
    return pl.pallas_call(
        functools.partial(_attn_prompt_kernel, tq=tq, tk=tk),
        grid=(n_batch, n_heads, n_q),
        in_specs=[
            pl.BlockSpec((tq, HEAD_DIM), lambda b, h, i: (b * n_q + i, h)),
            pl.BlockSpec((seq, HEAD_DIM), lambda b, h, i: (b, h)),
            pl.BlockSpec((seq, HEAD_DIM), lambda b, h, i: (b, h)),
            pl.BlockSpec((ROW_BLOCK, HEAD_DIM), lambda b, h, i: (meta_blk, h)),
            pl.BlockSpec((ROW_BLOCK, HEAD_DIM), lambda b, h, i: (meta_blk, h)),
            pl.BlockSpec(tri.shape, lambda b, h, i: (0, 0)),
        ],
        out_specs=pl.BlockSpec((tq, HEAD_DIM), lambda b, h, i: (b * n_q + i, h)),
        out_shape=jax.ShapeDtypeStruct((n_batch * seq, n_heads * HEAD_DIM), BF16),
        compiler_params=pltpu.CompilerParams(
            dimension_semantics=("arbitrary", "arbitrary", "arbitrary"), vmem_limit_bytes=V7X_VMEM_LIMIT),
        name="attn_prompt",
    )(q, kb, vb, kb, vb, tri)


def _attn_meta_kernel(q_ref, k_ref, v_ref, tri_ref, o_ref):
    n = q_ref.shape[0]
    row = lax.broadcasted_iota(jnp.int32, (n, n), 0)
    col = lax.broadcasted_iota(jnp.int32, (n, n), 1)
    mask = (col < row) & (col < N_META)
    carry = jnp.zeros((n, 1), F32)
    acc = jnp.zeros((n, HEAD_DIM), F32)
    _, acc = _sb_block(q_ref[...], k_ref[...], v_ref[...], carry, acc, tri_ref[...][:n, :n], mask)
    o_ref[...] = acc.astype(o_ref.dtype)


def _attn_meta(q, kb, vb, tri, *, n_heads, meta_blk):
    blk = pl.BlockSpec((ROW_BLOCK, HEAD_DIM), lambda h: (meta_blk, h))
    return pl.pallas_call(
        _attn_meta_kernel,
        grid=(n_heads,),
        in_specs=[blk, blk, blk, pl.BlockSpec(tri.shape, lambda h: (0, 0))],
        out_specs=pl.BlockSpec((ROW_BLOCK, HEAD_DIM), lambda h: (0, h)),
        out_shape=jax.ShapeDtypeStruct((ROW_BLOCK, n_heads * HEAD_DIM), BF16),
        compiler_params=pltpu.CompilerParams(dimension_semantics=("arbitrary",)),
        name="attn_meta",
    )(q, kb, vb, tri)


def _attn_sample_kernel(q_ref, nk_ref, nv_ref, ck_ref, cv_ref, tri_ref, o_ref, carry_ref, acc_ref,
                        *, n_heads, tk, n_chunks):
    t = pl.program_id(1)
    lq = q_ref.shape[0]
    tri = tri_ref[...]
    n_sub = ck_ref.shape[1] // tk

    for h in range(n_heads):
        lanes = slice(h * HEAD_DIM, (h + 1) * HEAD_DIM)
        q = q_ref[:, lanes]

        @pl.when(t == 0)
        def _():
            row = lax.broadcasted_iota(jnp.int32, (lq, lq), 0)
            col = lax.broadcasted_iota(jnp.int32, (lq, lq), 1)
            carry, acc = _sb_block(q, nk_ref[:, lanes], nv_ref[:, lanes],
                                   jnp.zeros((lq, 1), F32), jnp.zeros((lq, HEAD_DIM), F32),
                                   tri[:lq, :lq], col < row)
            carry_ref[h] = jnp.broadcast_to(carry, (lq, HEAD_DIM))
            acc_ref[h] = acc

        def body(s, state):
            off = pl.multiple_of((n_sub - 1 - s) * tk, tk)
            kb = ck_ref[0, pl.ds(off, tk), lanes].astype(BF16)
            vb = cv_ref[0, pl.ds(off, tk), lanes].astype(BF16)
            return _sb_block(q, kb, vb, state[0], state[1], tri, None)

        carry, acc = lax.fori_loop(0, n_sub, body, (carry_ref[h][:, :1], acc_ref[h]))
        carry_ref[h] = jnp.broadcast_to(carry, (lq, HEAD_DIM))
        acc_ref[h] = acc

        @pl.when(t == n_chunks - 1)
        def _():
            o_ref[:, lanes] = acc.astype(o_ref.dtype)


def _attn_sample(q, kb, vb, cache_k, cache_v, tri, *, layer, n_batch, lq, n_heads, row0, tk, chunk):
    past = cache_k.shape[1]
    width = n_heads * HEAD_DIM
    n_chunks = past // chunk
    blk0 = row0 // lq
    new_blk = pl.BlockSpec((lq, width), lambda b, t: (blk0 + b, 0))
    cache_blk = pl.BlockSpec((1, chunk, width), lambda b, t: (layer * n_batch + b, n_chunks - 1 - t, 0))
    return pl.pallas_call(
        functools.partial(_attn_sample_kernel, n_heads=n_heads, tk=tk, n_chunks=n_chunks),
        grid=(n_batch, n_chunks),
        in_specs=[new_blk, new_blk, new_blk, cache_blk, cache_blk,
                  pl.BlockSpec(tri.shape, lambda b, t: (0, 0))],
        out_specs=pl.BlockSpec((lq, width), lambda b, t: (b, 0)),
        out_shape=jax.ShapeDtypeStruct((n_batch * lq, width), BF16),
        scratch_shapes=[pltpu.VMEM((n_heads, lq, HEAD_DIM), F32), pltpu.VMEM((n_heads, lq, HEAD_DIM), F32)],
        compiler_params=pltpu.CompilerParams(
            dimension_semantics=("arbitrary", "arbitrary"), vmem_limit_bytes=V7X_VMEM_LIMIT),
        name="attn_sample",
    )(q, kb, vb, cache_k, cache_v, tri)


def _mixout_kernel(osb_ref, u_ref, halo_ref, h_ref, wp_ref, ps_ref, wo_ref, g_ref, o_ref, ext_ref, cat_ref,
                   *, meta_step, sb_width):
    n_seq, lt, pw = u_ref.shape
    cg = pw // len(POOL_WINDOWS)
    rows = n_seq * lt
    ext_ref[:, :HALO, :] = halo_ref[...]
    ext_ref[:, HALO:, :] = u_ref[...]
    cat_ref[:, :sb_width] = osb_ref[...]

    n_hist = jnp.where(pl.program_id(0) == meta_step, 0, POOL_HIST)
    n_before = (n_hist + lax.broadcasted_iota(jnp.int32, (1, lt, 1), 1)).astype(F32)

    for g, w in enumerate(POOL_WINDOWS):
        lanes = pl.ds(g * cg, cg)
        cur = ext_ref[:, pl.ds(HALO, lt), lanes]
        s = cur
        for back in range(1, w):
            s = s + ext_ref[:, pl.ds(HALO - back, lt), lanes]
        cnt = jnp.minimum(float(w), n_before + 1.0)
        dlt = (s / cnt - cur).reshape(rows, cg).astype(BF16)
        y = _dot(dlt, wp_ref[g]) * ps_ref[:, lanes]
        cat_ref[:, pl.ds(sb_width + g * cg, cg)] = y.astype(BF16)

    mixed = _dot(cat_ref[...], wo_ref[...])
    o_ref[...] = h_ref[...] + _rms(mixed, g_ref[...])


def _mixout(o_sb, u, halo, h, w_pool, pool_scale, w_out, g):
    m, d = h.shape
    pw = u.shape[1]
    sb_width = o_sb.shape[1]
    n_steps = m // ROW_BLOCK
    seq_per = ROW_BLOCK // SEQ_TILE
    u3 = u.reshape(m // SEQ_TILE, SEQ_TILE, pw)
    return pl.pallas_call(
        functools.partial(_mixout_kernel, meta_step=n_steps - 1, sb_width=sb_width),
        grid=(n_steps,),
        in_specs=[
            pl.BlockSpec((ROW_BLOCK, sb_width), lambda i: (i, 0)),
            pl.BlockSpec((seq_per, SEQ_TILE, pw), lambda i: (i, 0, 0)),
            pl.BlockSpec((seq_per, HALO, pw), lambda i: (i, 0, 0)),
            pl.BlockSpec((ROW_BLOCK, d), lambda i: (i, 0)),
            pl.BlockSpec(w_pool.shape, lambda i: (0, 0, 0)),
            pl.BlockSpec((1, pw), lambda i: (0, 0)),
            pl.BlockSpec(w_out.shape, lambda i: (0, 0)),
            pl.BlockSpec((1, d), lambda i: (0, 0)),
        ],
        out_specs=pl.BlockSpec((ROW_BLOCK, d), lambda i: (i, 0)),
        out_shape=jax.ShapeDtypeStruct((m, d), F32),
        scratch_shapes=[pltpu.VMEM((seq_per, HALO + SEQ_TILE, pw), F32),
                        pltpu.VMEM((ROW_BLOCK, sb_width + pw), BF16)],
        compiler_params=pltpu.CompilerParams(
            dimension_semantics=("arbitrary",), vmem_limit_bytes=V7X_VMEM_LIMIT),
        name="mixout",
    )(o_sb, u3, halo, h, w_pool, pool_scale, w_out, g)


def _row_tile(m, cap):
    blocks = m // ROW_BLOCK
    best = 1
    for dvs in range(1, blocks + 1):
        if blocks % dvs == 0 and dvs * ROW_BLOCK <= cap:
            best = dvs
    return best * ROW_BLOCK


def _col_tile(n, cap):
    best = HEAD_DIM
    for t in range(HEAD_DIM, cap + 1, HEAD_DIM):
        if n % t == 0:
            best = t
    return best


def kernel(x_prompt, x_sample, cache_k, cache_v, state_pool, meta_tokens, w_in, w_out, w_pool, pool_scale,
           norm_gains, ffn1_gate, ffn1_up, ffn1_down, ffn2_gate, ffn2_up, ffn2_down):
    n_b, seq, d = x_prompt.shape
    s_b, s_len, _ = x_sample.shape
    depth, _, past, n_heads, head_dim = cache_k.shape
    pw = state_pool.shape[-1]
    sbw = n_heads * head_dim
    assert head_dim == HEAD_DIM and meta_tokens.shape[0] == N_META and sbw == pw
    assert w_in.shape[-1] == 3 * sbw + pw and state_pool.shape[2] == POOL_HIST
    assert seq % SEQ_TILE == 0 and s_len == SEQ_TILE and seq >= POOL_HIST

    r_p = n_b * seq
    r_t = r_p + s_b * s_len
    assert r_t % ROW_BLOCK == 0
    m = r_t + ROW_BLOCK
    meta_blk = r_t // ROW_BLOCK

    tk = 256 if seq % 256 == 0 and past % 256 == 0 else 128
    tq = 128
    chunk = 1024 if past % 1024 == 0 else tk
    tri = (lax.broadcasted_iota(jnp.int32, (tk, tk), 0) > lax.broadcasted_iota(jnp.int32, (tk, tk), 1)).astype(BF16)

    x = jnp.concatenate([
        x_prompt.reshape(r_p, d), x_sample.reshape(s_b * s_len, d), meta_tokens.astype(F32),
        jnp.zeros((ROW_BLOCK - N_META, d), F32)], axis=0)

    ck = cache_k.reshape(depth * s_b, past, sbw)
    cv = cache_v.reshape(depth * s_b, past, sbw)
    tm_ffn = _row_tile(m, 576)
    tf = _col_tile(ffn1_gate.shape[-1], 512)
    tm_proj = _row_tile(m, 576)
    n_tiles_p = seq // SEQ_TILE

    outs = [[] for _ in range(6)]
    for l in range(depth):
        gains = norm_gains[l].astype(F32)
        gain = lambda n: gains[n][None, :]
        bf = lambda w: w[l].astype(BF16)

        x = _ffn(x, gain(0), gain(1), bf(ffn1_gate), bf(ffn1_up), bf(ffn1_down), tm=tm_ffn, tf=tf)
        q, kb, vb, k, v, u = _inproj(x, gain(2), bf(w_in), tm=tm_proj, width=sbw)

        o_p = _attn_prompt(q, kb, vb, tri, n_batch=n_b, seq=seq, n_heads=n_heads, meta_blk=meta_blk, tq=tq, tk=tk)
        o_s = _attn_sample(q, kb, vb, ck, cv, tri, layer=l, n_batch=s_b, lq=s_len, n_heads=n_heads,
                           row0=r_p, tk=tk, chunk=chunk)
        o_m = _attn_meta(q, kb, vb, tri, n_heads=n_heads, meta_blk=meta_blk)
        o_sb = jnp.concatenate([o_p, o_s, o_m], axis=0)

        u_p = u[:r_p].reshape(n_b, n_tiles_p, SEQ_TILE, pw)
        u_meta = jnp.broadcast_to(u[r_t:r_t + N_META][None, None], (n_b, 1, HALO, pw))
        halo_p = jnp.concatenate([u_meta, u_p[:, :-1, SEQ_TILE - HALO:]], axis=1).reshape(n_b * n_tiles_p, HALO, pw)
        halo_s = jnp.pad(state_pool[l].astype(F32), ((0, 0), (HALO - POOL_HIST, 0), (0, 0)))
        halo = jnp.concatenate([halo_p, halo_s, jnp.zeros((ROW_BLOCK // SEQ_TILE, HALO, pw), F32)], axis=0)

        x = _mixout(o_sb, u, halo, x, bf(w_pool), pool_scale[l].astype(F32)[None, :], bf(w_out), gain(3))
        x = _ffn(x, gain(4), gain(5), bf(ffn2_gate), bf(ffn2_up), bf(ffn2_down), tm=tm_ffn, tf=tf)

        def with_meta(a):
            meta = jnp.broadcast_to(a[r_t:r_t + N_META][None], (n_b, N_META, sbw))
            full = jnp.concatenate([meta, a[:r_p].reshape(n_b, seq, sbw)], axis=1)
            return full.reshape(n_b, N_META + seq, n_heads, HEAD_DIM)

        outs[0].append(with_meta(k))
        outs[1].append(with_meta(v))
        outs[2].append(u[:r_p].reshape(n_b, seq, pw)[:, seq - POOL_HIST:])
        outs[3].append(k[r_p:r_t].reshape(s_b, s_len, n_heads, HEAD_DIM))
        outs[4].append(v[r_p:r_t].reshape(s_b, s_len, n_heads, HEAD_DIM))
        outs[5].append(u[r_p:r_t].reshape(s_b, s_len, pw)[:, s_len - POOL_HIST:])

    y_prompt = x[:r_p].reshape(n_b, seq, d)
    y_sample = x[r_p:r_t].reshape(s_b, s_len, d)
    return (y_prompt, y_sample) + tuple(jnp.stack(o) for o in outs)
```

```python
import functools

import jax
import jax.numpy as jnp
from jax import lax
from jax.experimental import pallas as pl
from jax.experimental.pallas import tpu as pltpu

F32 = jnp.float32
BF16 = jnp.bfloat16

RMS_EPS = 1e-6
HEAD_DIM = 128
MXU_WIDTH = 256
N_META = 16
POOL_WINDOWS = (2, 4, 8, 16)
POOL_HIST = max(POOL_WINDOWS) - 1
HALO = 16
SEQ_TILE = 64
ROW_BLOCK = 128
V7X_VMEM_LIMIT = 56 * 1024 * 1024


def _rms(x, g):
    ms = jnp.mean(x * x, axis=-1, keepdims=True)
    return x * lax.rsqrt(ms + RMS_EPS) * g


def _dot(a, b):
    return jnp.dot(a, b, preferred_element_type=F32)


def _ffn_kernel(x_ref, gi_ref, go_ref, wg_ref, wu_ref, wd_ref, o_ref, xn_ref, *, n_f):
    j = pl.program_id(1)

    @pl.when(j == 0)
    def _():
        xn_ref[...] = _rms(x_ref[...], gi_ref[...]).astype(BF16)
        o_ref[...] = jnp.zeros_like(o_ref)

    xn = xn_ref[...]
    tf = wg_ref.shape[1]
    hidden = []
    for c in range(0, tf, MXU_WIDTH):
        g = _dot(xn, wg_ref[:, c:c + MXU_WIDTH].astype(BF16))
        u = _dot(xn, wu_ref[:, c:c + MXU_WIDTH].astype(BF16))
        hidden.append((g * jax.nn.sigmoid(g) * u).astype(BF16))
    o_ref[...] += _dot(jnp.concatenate(hidden, axis=1), wd_ref[...].astype(BF16))

    @pl.when(j == n_f - 1)
    def _():
        o_ref[...] = x_ref[...] + 0.5 * _rms(o_ref[...], go_ref[...])


def _ffn(x, gi, go, wg, wu, wd, *, layer, tm, tf):
    m, d = x.shape
    f = wg.shape[2]
    n_f = f // tf
    return pl.pallas_call(
        functools.partial(_ffn_kernel, n_f=n_f),
        grid=(m // tm, n_f),
        in_specs=[
            pl.BlockSpec((tm, d), lambda i, j: (i, 0), pipeline_mode=pl.Buffered(1)),
            pl.BlockSpec((1, d), lambda i, j: (0, 0)),
            pl.BlockSpec((1, d), lambda i, j: (0, 0)),
            pl.BlockSpec((None, d, tf), lambda i, j: (layer, 0, j)),
            pl.BlockSpec((None, d, tf), lambda i, j: (layer, 0, j)),
            pl.BlockSpec((None, tf, d), lambda i, j: (layer, j, 0)),
        ],
        out_specs=pl.BlockSpec((tm, d), lambda i, j: (i, 0)),
        out_shape=jax.ShapeDtypeStruct((m, d), F32),
        scratch_shapes=[pltpu.VMEM((tm, d), BF16)],
        compiler_params=pltpu.CompilerParams(
            dimension_semantics=("arbitrary", "arbitrary"), vmem_limit_bytes=V7X_VMEM_LIMIT),
        name="ffn",
    )(x, gi, go, wg, wu, wd)


def _inproj_kernel(x_ref, g_ref, w_ref, q_ref, kb_ref, vb_ref, k_ref, v_ref, u_ref, xn_ref, *, q_scale):
    j = pl.program_id(1)

    @pl.when(j == 0)
    def _():
        xn_ref[...] = _rms(x_ref[...], g_ref[...]).astype(BF16)

    proj = _dot(xn_ref[...], w_ref[...])

    @pl.when(j == 0)
    def _():
        q_ref[...] = (proj * q_scale).astype(BF16)

    @pl.when(j == 1)
    def _():
        k_ref[...] = proj
        kb_ref[...] = proj.astype(BF16)

    @pl.when(j == 2)
    def _():
        v_ref[...] = proj
        vb_ref[...] = proj.astype(BF16)

    @pl.when(j == 3)
    def _():
        u_ref[...] = proj


def _inproj(x, g, w_in, *, tm, width):
    m, d = x.shape
    row_blk = lambda i, j: (i, 0)
    out_bf = jax.ShapeDtypeStruct((m, width), BF16)
    out_f = jax.ShapeDtypeStruct((m, width), F32)
    return pl.pallas_call(
        functools.partial(_inproj_kernel, q_scale=HEAD_DIM ** -0.5),
        grid=(m // tm, 4),
        in_specs=[
            pl.BlockSpec((tm, d), row_blk),
            pl.BlockSpec((1, d), lambda i, j: (0, 0)),
            pl.BlockSpec((d, width), lambda i, j: (0, j)),
        ],
        out_specs=[pl.BlockSpec((tm, width), row_blk)] * 6,
        out_shape=[out_bf, out_bf, out_bf, out_f, out_f, out_f],
        scratch_shapes=[pltpu.VMEM((tm, d), BF16)],
        compiler_params=pltpu.CompilerParams(
            dimension_semantics=("arbitrary", "arbitrary"), vmem_limit_bytes=V7X_VMEM_LIMIT),
        name="inproj",
    )(x, g, w_in)


def _lanes(h):
    return slice(h * HEAD_DIM, (h + 1) * HEAD_DIM)


def _sb_step(qs, ks, vs, carry_ref, acc_ref, tri, mask):
    tq = qs[0].shape[0]
    z = jnp.concatenate(
        [lax.dot_general(q, k, (((1,), (1,)), ((), ())), preferred_element_type=F32) for q, k in zip(qs, ks)],
        axis=0)
    sp = jnp.maximum(z, 0.0) + jnp.log(1.0 + jnp.exp(-jnp.abs(z)))
    log_beta = z - sp
    if mask is not None:
        sp = jnp.where(mask, sp, 0.0)
    hi = sp.astype(BF16)
    lo = (sp - hi.astype(F32)).astype(BF16)
    after = _dot(jnp.concatenate([hi, lo], axis=1), jnp.concatenate([tri, tri], axis=0))
    p = jnp.exp(log_beta - after)
    if mask is not None:
        p = jnp.where(mask, p, 0.0)
    pb = p.astype(BF16)
    pv = jnp.concatenate([_dot(pb[h * tq:(h + 1) * tq], v) for h, v in enumerate(vs)], axis=0)
    carry = carry_ref[...]
    acc_ref[...] += jnp.exp(-carry) * pv
    carry_ref[...] = carry + jnp.sum(sp, axis=1, keepdims=True)


def _attn_prompt_kernel(q_ref, k_ref, v_ref, mk_ref, mv_ref, tri_ref, o_ref, carry_ref, acc_ref, *, t, hg):
    i = pl.program_id(2)
    tri = tri_ref[...]
    qs = [q_ref[:, _lanes(h)] for h in range(hg)]
    carry_ref[...] = jnp.zeros_like(carry_ref)
    acc_ref[...] = jnp.zeros_like(acc_ref)

    def kv(c):
        off = pl.multiple_of(c * t, t)
        return ([k_ref[pl.ds(off, t), _lanes(h)] for h in range(hg)],
                [v_ref[pl.ds(off, t), _lanes(h)] for h in range(hg)])

    row = lax.broadcasted_iota(jnp.int32, (hg * t, t), 0) & (t - 1)
    col = lax.broadcasted_iota(jnp.int32, (hg * t, t), 1)
    ks, vs = kv(i)
    _sb_step(qs, ks, vs, carry_ref, acc_ref, tri, col < row)

    def body(s, _):
        ks, vs = kv(i - 1 - s)
        _sb_step(qs, ks, vs, carry_ref, acc_ref, tri, None)
        return 0

    lax.fori_loop(0, i, body, 0)

    n_mk = mk_ref.shape[0]
    mcol = lax.broadcasted_iota(jnp.int32, (hg * t, n_mk), 1)
    _sb_step(qs, [mk_ref[:, _lanes(h)] for h in range(hg)], [mv_ref[:, _lanes(h)] for h in range(hg)],
             carry_ref, acc_ref, tri[:n_mk, :n_mk], mcol < N_META)
    for h in range(hg):
        o_ref[:, _lanes(h)] = acc_ref[h * t:(h + 1) * t, :].astype(o_ref.dtype)


def _attn_prompt(q, kb, vb, tri, *, n_batch, seq, n_heads, meta_blk, hg):
    t = tri.shape[0]
    assert t & (t - 1) == 0 and seq % t == 0 and n_heads % hg == 0
    n_q = seq // t
    w = hg * HEAD_DIM
    return pl.pallas_call(
        functools.partial(_attn_prompt_kernel, t=t, hg=hg),
        grid=(n_batch, n_heads // hg, n_q),
        in_specs=[
            pl.BlockSpec((t, w), lambda b, g, i: (b * n_q + i, g)),
            pl.BlockSpec((seq, w), lambda b, g, i: (b, g)),
            pl.BlockSpec((seq, w), lambda b, g, i: (b, g)),
            pl.BlockSpec((ROW_BLOCK, w), lambda b, g, i: (meta_blk, g)),
            pl.BlockSpec((ROW_BLOCK, w), lambda b, g, i: (meta_blk, g)),
            pl.BlockSpec(tri.shape, lambda b, g, i: (0, 0)),
        ],
        out_specs=pl.BlockSpec((t, w), lambda b, g, i: (b * n_q + i, g)),
        out_shape=jax.ShapeDtypeStruct((n_batch * seq, n_heads * HEAD_DIM), BF16),
        scratch_shapes=[pltpu.VMEM((hg * t, 1), F32), pltpu.VMEM((hg * t, HEAD_DIM), F32)],
        compiler_params=pltpu.CompilerParams(
            dimension_semantics=("arbitrary", "arbitrary", "arbitrary"), vmem_limit_bytes=V7X_VMEM_LIMIT),
        name="attn_prompt",
    )(q, kb, vb, kb, vb, tri)


def _attn_meta_kernel(q_ref, k_ref, v_ref, tri_ref, o_ref, carry_ref, acc_ref, *, n_heads):
    n = q_ref.shape[0]
    carry_ref[...] = jnp.zeros_like(carry_ref)
    acc_ref[...] = jnp.zeros_like(acc_ref)
    row = lax.broadcasted_iota(jnp.int32, (n_heads * n, n), 0) & (n - 1)
    col = lax.broadcasted_iota(jnp.int32, (n_heads * n, n), 1)
    mask = (col < row) & (col < N_META)
    heads = range(n_heads)
    _sb_step([q_ref[:, _lanes(h)] for h in heads], [k_ref[:, _lanes(h)] for h in heads],
             [v_ref[:, _lanes(h)] for h in heads], carry_ref, acc_ref, tri_ref[...][:n, :n], mask)
    for h in heads:
        o_ref[:, _lanes(h)] = acc_ref[h * n:(h + 1) * n, :].astype(o_ref.dtype)


def _attn_meta(q, kb, vb, tri, *, n_heads, meta_blk):
    w = n_heads * HEAD_DIM
    blk = pl.BlockSpec((ROW_BLOCK, w), lambda i: (meta_blk, 0))
    return pl.pallas_call(
        functools.partial(_attn_meta_kernel, n_heads=n_heads),
        grid=(1,),
        in_specs=[blk, blk, blk, pl.BlockSpec(tri.shape, lambda i: (0, 0))],
        out_specs=pl.BlockSpec((ROW_BLOCK, w), lambda i: (0, 0)),
        out_shape=jax.ShapeDtypeStruct((ROW_BLOCK, w), BF16),
        scratch_shapes=[pltpu.VMEM((n_heads * ROW_BLOCK, 1), F32), pltpu.VMEM((n_heads * ROW_BLOCK, HEAD_DIM), F32)],
        compiler_params=pltpu.CompilerParams(dimension_semantics=("arbitrary",)),
        name="attn_meta",
    )(q, kb, vb, tri)


def _attn_sample_kernel(q_ref, nk_ref, nv_ref, ck_ref, cv_ref, tri_ref, o_ref, carry_ref, acc_ref,
                        *, n_heads, tk, n_chunks):
    t = pl.program_id(1)
    lq = q_ref.shape[0]
    heads = range(n_heads)
    tri = tri_ref[...]
    qs = [q_ref[:, _lanes(h)] for h in heads]

    @pl.when(t == 0)
    def _():
        carry_ref[...] = jnp.zeros_like(carry_ref)
        acc_ref[...] = jnp.zeros_like(acc_ref)
        row = lax.broadcasted_iota(jnp.int32, (n_heads * lq, lq), 0) & (lq - 1)
        col = lax.broadcasted_iota(jnp.int32, (n_heads * lq, lq), 1)
        _sb_step(qs, [nk_ref[:, _lanes(h)] for h in heads], [nv_ref[:, _lanes(h)] for h in heads],
                 carry_ref, acc_ref, tri[:lq, :lq], col < row)

    n_sub = ck_ref.shape[0] // (tk * n_heads)
    for s in reversed(range(n_sub)):
        head_rows = lambda h: pl.ds(s * tk * n_heads + h, tk, stride=n_heads)
        _sb_step(qs, [ck_ref[head_rows(h), :].astype(BF16) for h in heads],
                 [cv_ref[head_rows(h), :].astype(BF16) for h in heads], carry_ref, acc_ref, tri, None)

    @pl.when(t == n_chunks - 1)
    def _():
        for h in heads:
            o_ref[:, _lanes(h)] = acc_ref[h * lq:(h + 1) * lq, :].astype(o_ref.dtype)


def _attn_sample(q, kb, vb, cache_k, cache_v, tri, *, layer, lq, row0, chunk):
    _, n_batch, past, n_heads, _ = cache_k.shape
    assert lq & (lq - 1) == 0
    tk = tri.shape[0]
    width = n_heads * HEAD_DIM
    n_chunks = past // chunk
    blk0 = row0 // lq
    new_blk = pl.BlockSpec((lq, width), lambda b, t: (blk0 + b, 0))
    cache_rows = lambda c: c.reshape(-1, HEAD_DIM)
    cache_blk = pl.BlockSpec((chunk * n_heads, HEAD_DIM),
                             lambda b, t: ((layer * n_batch + b) * n_chunks + n_chunks - 1 - t, 0))
    return pl.pallas_call(
        functools.partial(_attn_sample_kernel, n_heads=n_heads, tk=tk, n_chunks=n_chunks),
        grid=(n_batch, n_chunks),
        in_specs=[new_blk, new_blk, new_blk, cache_blk, cache_blk, pl.BlockSpec(tri.shape, lambda b, t: (0, 0))],
        out_specs=pl.BlockSpec((lq, width), lambda b, t: (b, 0)),
        out_shape=jax.ShapeDtypeStruct((n_batch * lq, width), BF16),
        scratch_shapes=[pltpu.VMEM((n_heads * lq, 1), F32), pltpu.VMEM((n_heads * lq, HEAD_DIM), F32)],
        compiler_params=pltpu.CompilerParams(
            dimension_semantics=("arbitrary", "arbitrary"), vmem_limit_bytes=V7X_VMEM_LIMIT),
        name="attn_sample",
    )(q, kb, vb, cache_rows(cache_k), cache_rows(cache_v), tri)


def _mixout_kernel(osb_ref, u_ref, halo_ref, h_ref, wp_ref, ps_ref, wo_ref, g_ref, o_ref, ext_ref, cat_ref,
                   *, meta_step, sb_width):
    n_seq, lt, pw = u_ref.shape
    cg = pw // len(POOL_WINDOWS)
    rows = n_seq * lt
    ext_ref[:, :HALO, :] = halo_ref[...]
    ext_ref[:, HALO:, :] = u_ref[...]
    cat_ref[:, :sb_width] = osb_ref[...]

    n_hist = jnp.where(pl.program_id(0) == meta_step, 0, POOL_HIST)
    n_before = (n_hist + lax.broadcasted_iota(jnp.int32, (1, lt, 1), 1)).astype(F32)

    for g, w in enumerate(POOL_WINDOWS):
        lanes = pl.ds(g * cg, cg)
        cur = ext_ref[:, pl.ds(HALO, lt), lanes]
        s = cur
        for back in range(1, w):
            s = s + ext_ref[:, pl.ds(HALO - back, lt), lanes]
        cnt = jnp.minimum(float(w), n_before + 1.0)
        dlt = (s / cnt - cur).reshape(rows, cg).astype(BF16)
        y = _dot(dlt, wp_ref[g]) * ps_ref[:, lanes]
        cat_ref[:, pl.ds(sb_width + g * cg, cg)] = y.astype(BF16)

    mixed = _dot(cat_ref[...], wo_ref[...])
    o_ref[...] = h_ref[...] + _rms(mixed, g_ref[...])


def _mixout(o_sb, u, halo, h, w_pool, pool_scale, w_out, g):
    m, d = h.shape
    pw = u.shape[1]
    sb_width = o_sb.shape[1]
    n_steps = m // ROW_BLOCK
    seq_per = ROW_BLOCK // SEQ_TILE
    u3 = u.reshape(m // SEQ_TILE, SEQ_TILE, pw)
    return pl.pallas_call(
        functools.partial(_mixout_kernel, meta_step=n_steps - 1, sb_width=sb_width),
        grid=(n_steps,),
        in_specs=[
            pl.BlockSpec((ROW_BLOCK, sb_width), lambda i: (i, 0)),
            pl.BlockSpec((seq_per, SEQ_TILE, pw), lambda i: (i, 0, 0)),
            pl.BlockSpec((seq_per, HALO, pw), lambda i: (i, 0, 0)),
            pl.BlockSpec((ROW_BLOCK, d), lambda i: (i, 0)),
            pl.BlockSpec(w_pool.shape, lambda i: (0, 0, 0)),
            pl.BlockSpec((1, pw), lambda i: (0, 0)),
            pl.BlockSpec(w_out.shape, lambda i: (0, 0)),
            pl.BlockSpec((1, d), lambda i: (0, 0)),
        ],
        out_specs=pl.BlockSpec((ROW_BLOCK, d), lambda i: (i, 0)),
        out_shape=jax.ShapeDtypeStruct((m, d), F32),
        scratch_shapes=[pltpu.VMEM((seq_per, HALO + SEQ_TILE, pw), F32),
                        pltpu.VMEM((ROW_BLOCK, sb_width + pw), BF16)],
        compiler_params=pltpu.CompilerParams(
            dimension_semantics=("arbitrary",), vmem_limit_bytes=V7X_VMEM_LIMIT),
        name="mixout",
    )(o_sb, u3, halo, h, w_pool, pool_scale, w_out, g)


def _row_tile(m, cap):
    blocks = m // ROW_BLOCK
    best = 1
    for dvs in range(1, blocks + 1):
        if blocks % dvs == 0 and dvs * ROW_BLOCK <= cap:
            best = dvs
    return best * ROW_BLOCK


def _col_tile(n, cap):
    best = HEAD_DIM
    for t in range(HEAD_DIM, cap + 1, HEAD_DIM):
        if n % t == 0:
            best = t
    return best


def kernel(x_prompt, x_sample, cache_k, cache_v, state_pool, meta_tokens, w_in, w_out, w_pool, pool_scale,
           norm_gains, ffn1_gate, ffn1_up, ffn1_down, ffn2_gate, ffn2_up, ffn2_down):
    n_b, seq, d = x_prompt.shape
    s_b, s_len, _ = x_sample.shape
    depth, _, past, n_heads, head_dim = cache_k.shape
    pw = state_pool.shape[-1]
    sbw = n_heads * head_dim
    assert head_dim == HEAD_DIM and meta_tokens.shape[0] == N_META and sbw == pw
    assert w_in.shape[-1] == 3 * sbw + pw and state_pool.shape[2] == POOL_HIST
    assert seq % SEQ_TILE == 0 and s_len == SEQ_TILE and seq >= POOL_HIST

    r_p = n_b * seq
    r_t = r_p + s_b * s_len
    assert r_t % ROW_BLOCK == 0
    m = r_t + ROW_BLOCK
    meta_blk = r_t // ROW_BLOCK

    tk = 256
    assert seq % tk == 0 and past % tk == 0
    chunk = 1024 if past % 1024 == 0 else tk
    hg = 4 if n_heads % 4 == 0 else 1
    tri = (lax.broadcasted_iota(jnp.int32, (tk, tk), 0) > lax.broadcasted_iota(jnp.int32, (tk, tk), 1)).astype(BF16)

    x = jnp.concatenate([
        x_prompt.reshape(r_p, d), x_sample.reshape(s_b * s_len, d), meta_tokens.astype(F32),
        jnp.zeros((ROW_BLOCK - N_META, d), F32)], axis=0)

    tm_ffn = _row_tile(m, 1152)
    tf = _col_tile(ffn1_gate.shape[-1], 256)
    tm_proj = _row_tile(m, 576)
    n_tiles_p = seq // SEQ_TILE

    outs = [[] for _ in range(6)]
    for l in range(depth):
        gains = norm_gains[l].astype(F32)
        gain = lambda n: gains[n][None, :]
        bf = lambda w: w[l].astype(BF16)

        x = _ffn(x, gain(0), gain(1), ffn1_gate, ffn1_up, ffn1_down, layer=l, tm=tm_ffn, tf=tf)
        q, kb, vb, k, v, u = _inproj(x, gain(2), bf(w_in), tm=tm_proj, width=sbw)

        o_p = _attn_prompt(q, kb, vb, tri, n_batch=n_b, seq=seq, n_heads=n_heads, meta_blk=meta_blk, hg=hg)
        o_s = _attn_sample(q, kb, vb, cache_k, cache_v, tri, layer=l, lq=s_len, row0=r_p, chunk=chunk)
        o_m = _attn_meta(q, kb, vb, tri, n_heads=n_heads, meta_blk=meta_blk)
        o_sb = jnp.concatenate([o_p, o_s, o_m], axis=0)

        u_p = u[:r_p].reshape(n_b, n_tiles_p, SEQ_TILE, pw)
        u_meta = jnp.broadcast_to(u[r_t:r_t + N_META][None, None], (n_b, 1, HALO, pw))
        halo_p = jnp.concatenate([u_meta, u_p[:, :-1, SEQ_TILE - HALO:]], axis=1).reshape(n_b * n_tiles_p, HALO, pw)
        halo_s = jnp.pad(state_pool[l].astype(F32), ((0, 0), (HALO - POOL_HIST, 0), (0, 0)))
        halo = jnp.concatenate([halo_p, halo_s, jnp.zeros((ROW_BLOCK // SEQ_TILE, HALO, pw), F32)], axis=0)

        x = _mixout(o_sb, u, halo, x, bf(w_pool), pool_scale[l].astype(F32)[None, :], bf(w_out), gain(3))
        x = _ffn(x, gain(4), gain(5), ffn2_gate, ffn2_up, ffn2_down, layer=l, tm=tm_ffn, tf=tf)

        def with_meta(a):
            meta = jnp.broadcast_to(a[r_t:r_t + N_META][None], (n_b, N_META, sbw))
            full = jnp.concatenate([meta, a[:r_p].reshape(n_b, seq, sbw)], axis=1)
            return full.reshape(n_b, N_META + seq, n_heads, HEAD_DIM)

        outs[0].append(with_meta(k))
        outs[1].append(with_meta(v))
        outs[2].append(u[:r_p].reshape(n_b, seq, pw)[:, seq - POOL_HIST:])
        outs[3].append(k[r_p:r_t].reshape(s_b, s_len, n_heads, HEAD_DIM))
        outs[4].append(v[r_p:r_t].reshape(s_b, s_len, n_heads, HEAD_DIM))
        outs[5].append(u[r_p:r_t].reshape(s_b, s_len, pw)[:, s_len - POOL_HIST:])

    y_prompt = x[:r_p].reshape(n_b, seq, d)
    y_sample = x[r_p:r_t].reshape(s_b, s_len, d)
    return (y_prompt, y_sample) + tuple(jnp.stack(o) for o in outs)
```

```python
import functools

import jax
import jax.numpy as jnp
from jax import lax
from jax.experimental import pallas as pl
from jax.experimental.pallas import tpu as pltpu

F32 = jnp.float32
BF16 = jnp.bfloat16

RMS_EPS = 1e-6
HEAD_DIM = 128
MXU_WIDTH = 256
N_META = 16
POOL_WINDOWS = (2, 4, 8, 16)
POOL_HIST = max(POOL_WINDOWS) - 1
HALO = 16
SEQ_TILE = 64
ROW_BLOCK = 128
V7X_VMEM_LIMIT = 56 * 1024 * 1024


def _rms(x, g):
    ms = jnp.mean(x * x, axis=-1, keepdims=True)
    return x * lax.rsqrt(ms + RMS_EPS) * g


def _dot(a, b):
    return jnp.dot(a, b, preferred_element_type=F32)


def _ffn_kernel(x_ref, gi_ref, go_ref, wg_ref, wu_ref, wd_ref, o_ref, xn_ref, *, n_f):
    j = pl.program_id(1)

    @pl.when(j == 0)
    def _():
        xn_ref[...] = _rms(x_ref[...], gi_ref[...]).astype(BF16)
        o_ref[...] = jnp.zeros_like(o_ref)

    xn = xn_ref[...]
    tf = wg_ref.shape[1]
    hidden = []
    for c in range(0, tf, MXU_WIDTH):
        g = _dot(xn, wg_ref[:, c:c + MXU_WIDTH].astype(BF16))
        u = _dot(xn, wu_ref[:, c:c + MXU_WIDTH].astype(BF16))
        hidden.append((g * jax.nn.sigmoid(g) * u).astype(BF16))
    o_ref[...] += _dot(jnp.concatenate(hidden, axis=1), wd_ref[...].astype(BF16))

    @pl.when(j == n_f - 1)
    def _():
        o_ref[...] = x_ref[...] + 0.5 * _rms(o_ref[...], go_ref[...])


def _ffn(x, gi, go, wg, wu, wd, *, layer, tm, tf):
    m, d = x.shape
    f = wg.shape[2]
    n_f = f // tf
    return pl.pallas_call(
        functools.partial(_ffn_kernel, n_f=n_f),
        grid=(m // tm, n_f),
        in_specs=[
            pl.BlockSpec((tm, d), lambda i, j: (i, 0), pipeline_mode=pl.Buffered(1)),
            pl.BlockSpec((1, d), lambda i, j: (0, 0)),
            pl.BlockSpec((1, d), lambda i, j: (0, 0)),
            pl.BlockSpec((None, d, tf), lambda i, j: (layer, 0, j)),
            pl.BlockSpec((None, d, tf), lambda i, j: (layer, 0, j)),
            pl.BlockSpec((None, tf, d), lambda i, j: (layer, j, 0)),
        ],
        out_specs=pl.BlockSpec((tm, d), lambda i, j: (i, 0)),
        out_shape=jax.ShapeDtypeStruct((m, d), F32),
        scratch_shapes=[pltpu.VMEM((tm, d), BF16)],
        compiler_params=pltpu.CompilerParams(
            dimension_semantics=("arbitrary", "arbitrary"), vmem_limit_bytes=V7X_VMEM_LIMIT),
        name="ffn",
    )(x, gi, go, wg, wu, wd)


def _inproj_kernel(x_ref, g_ref, w_ref, q_ref, kb_ref, vb_ref, k_ref, v_ref, u_ref, xn_ref, *, q_scale, n_heads):
    j = pl.program_id(1)
    tm = x_ref.shape[0]

    @pl.when(j == 0)
    def _():
        xn_ref[...] = _rms(x_ref[...], g_ref[...]).astype(BF16)

    proj = _dot(xn_ref[...], w_ref[...].astype(BF16))

    def store_heads(ref):
        for h in range(n_heads):
            ref[pl.ds(h, tm, stride=n_heads), :] = proj[:, _lanes(h)]

    @pl.when(j == 0)
    def _():
        q_ref[...] = (proj * q_scale).astype(BF16)

    @pl.when(j == 1)
    def _():
        store_heads(k_ref)
        kb_ref[...] = proj.astype(BF16)

    @pl.when(j == 2)
    def _():
        store_heads(v_ref)
        vb_ref[...] = proj.astype(BF16)

    @pl.when(j == 3)
    def _():
        u_ref[...] = proj


def _inproj(x, g, w_in, *, layer, tm, width):
    m, d = x.shape
    n_heads = width // HEAD_DIM
    row_blk = lambda i, j: (i, 0)
    wide = pl.BlockSpec((tm, width), row_blk)
    by_head = pl.BlockSpec((tm * n_heads, HEAD_DIM), row_blk)
    out_bf = jax.ShapeDtypeStruct((m, width), BF16)
    out_heads = jax.ShapeDtypeStruct((m * n_heads, HEAD_DIM), F32)
    return pl.pallas_call(
        functools.partial(_inproj_kernel, q_scale=HEAD_DIM ** -0.5, n_heads=n_heads),
        grid=(m // tm, 4),
        in_specs=[
            pl.BlockSpec((tm, d), row_blk),
            pl.BlockSpec((1, d), lambda i, j: (0, 0)),
            pl.BlockSpec((None, d, width), lambda i, j: (layer, 0, j)),
        ],
        out_specs=[wide, wide, wide, by_head, by_head, wide],
        out_shape=[out_bf, out_bf, out_bf, out_heads, out_heads, jax.ShapeDtypeStruct((m, width), F32)],
        scratch_shapes=[pltpu.VMEM((tm, d), BF16)],
        compiler_params=pltpu.CompilerParams(
            dimension_semantics=("arbitrary", "arbitrary"), vmem_limit_bytes=V7X_VMEM_LIMIT),
        name="inproj",
    )(x, g, w_in)


def _lanes(h):
    return slice(h * HEAD_DIM, (h + 1) * HEAD_DIM)


def _sb_block(qs, ks, vs, tri, mask):
    return _sb_blocks(qs, [(ks, vs)], tri, mask)[0]


def _sb_blocks(qs, kvs, tri, mask):
    tq = qs[0].shape[0]
    tri2 = jnp.concatenate([tri, tri], axis=0)
    zs = [jnp.concatenate(
        [lax.dot_general(q, k, (((1,), (1,)), ((), ())), preferred_element_type=F32) for q, k in zip(qs, ks)],
        axis=0) for ks, _ in kvs]
    suffixes = []
    for z in zs:
        sp = jnp.maximum(z, 0.0) + jnp.log(1.0 + jnp.exp(-jnp.abs(z)))
        if mask is not None:
            sp = jnp.where(mask, sp, 0.0)
        hi = sp.astype(BF16)
        lo = (sp - hi.astype(F32)).astype(BF16)
        suffixes.append(_dot(jnp.concatenate([hi, lo], axis=1), tri2))
    out = []
    for z, suffix, (_, vs) in zip(zs, suffixes, kvs):
        p = jnp.exp(z - suffix)
        if mask is not None:
            p = jnp.where(mask, p, 0.0)
        pb = p.astype(BF16)
        pv = jnp.concatenate([_dot(pb[h * tq:(h + 1) * tq], v) for h, v in enumerate(vs)], axis=0)
        out.append((pv, suffix[:, :1]))
    return out


def _sb_accumulate(carry_ref, acc_ref, blocks):
    carry = carry_ref[...]
    acc = acc_ref[...]
    for pv, row_sum in blocks:
        acc = acc + jnp.exp(-carry) * pv
        carry = carry + row_sum
    carry_ref[...] = carry
    acc_ref[...] = acc


def _sb_step(qs, ks, vs, carry_ref, acc_ref, tri, mask):
    _sb_accumulate(carry_ref, acc_ref, [_sb_block(qs, ks, vs, tri, mask)])


def _attn_prompt_kernel(q_ref, k_ref, v_ref, mk_ref, mv_ref, tri_ref, o_ref, carry_ref, acc_ref, *, t, hg):
    i = pl.program_id(2)
    tri = tri_ref[...]
    qs = [q_ref[:, _lanes(h)] for h in range(hg)]
    carry_ref[...] = jnp.zeros_like(carry_ref)
    acc_ref[...] = jnp.zeros_like(acc_ref)

    def kv(c):
        off = pl.multiple_of(c * t, t)
        return ([k_ref[pl.ds(off, t), _lanes(h)] for h in range(hg)],
                [v_ref[pl.ds(off, t), _lanes(h)] for h in range(hg)])

    row = lax.broadcasted_iota(jnp.int32, (hg * t, t), 0) & (t - 1)
    col = lax.broadcasted_iota(jnp.int32, (hg * t, t), 1)
    ks, vs = kv(i)
    _sb_step(qs, ks, vs, carry_ref, acc_ref, tri, col < row)

    def pair(s, _):
        _sb_accumulate(carry_ref, acc_ref,
                       _sb_blocks(qs, [kv(i - 1 - 2 * s), kv(i - 2 - 2 * s)], tri, None))
        return 0

    lax.fori_loop(0, i // 2, pair, 0)

    @pl.when(i % 2 == 1)
    def _():
        ks, vs = kv(0)
        _sb_step(qs, ks, vs, carry_ref, acc_ref, tri, None)

    n_mk = mk_ref.shape[0]
    mcol = lax.broadcasted_iota(jnp.int32, (hg * t, n_mk), 1)
    _sb_step(qs, [mk_ref[:, _lanes(h)] for h in range(hg)], [mv_ref[:, _lanes(h)] for h in range(hg)],
             carry_ref, acc_ref, tri[:n_mk, :n_mk], mcol < N_META)
    for h in range(hg):
        o_ref[:, _lanes(h)] = acc_ref[h * t:(h + 1) * t, :].astype(o_ref.dtype)


def _attn_prompt(q, kb, vb, tri, *, n_batch, seq, n_heads, meta_blk, hg):
    t = tri.shape[0]
    assert t & (t - 1) == 0 and seq % t == 0 and n_heads % hg == 0
    n_q = seq // t
    w = hg * HEAD_DIM
    return pl.pallas_call(
        functools.partial(_attn_prompt_kernel, t=t, hg=hg),
        grid=(n_batch, n_heads // hg, n_q),
        in_specs=[
            pl.BlockSpec((t, w), lambda b, g, i: (b * n_q + i, g)),
            pl.BlockSpec((seq, w), lambda b, g, i: (b, g)),
            pl.BlockSpec((seq, w), lambda b, g, i: (b, g)),
            pl.BlockSpec((ROW_BLOCK, w), lambda b, g, i: (meta_blk, g)),
            pl.BlockSpec((ROW_BLOCK, w), lambda b, g, i: (meta_blk, g)),
            pl.BlockSpec(tri.shape, lambda b, g, i: (0, 0)),
        ],
        out_specs=pl.BlockSpec((t, w), lambda b, g, i: (b * n_q + i, g)),
        out_shape=jax.ShapeDtypeStruct(q.shape, BF16),
        scratch_shapes=[pltpu.VMEM((hg * t, 1), F32), pltpu.VMEM((hg * t, HEAD_DIM), F32)],
        compiler_params=pltpu.CompilerParams(
            dimension_semantics=("arbitrary", "arbitrary", "arbitrary"), vmem_limit_bytes=V7X_VMEM_LIMIT),
        name="attn_prompt",
    )(q, kb, vb, kb, vb, tri)


def _attn_meta_kernel(q_ref, k_ref, v_ref, tri_ref, _o_in_ref, o_ref, carry_ref, acc_ref, *, n_heads):
    n = q_ref.shape[0]
    carry_ref[...] = jnp.zeros_like(carry_ref)
    acc_ref[...] = jnp.zeros_like(acc_ref)
    row = lax.broadcasted_iota(jnp.int32, (n_heads * n, n), 0) & (n - 1)
    col = lax.broadcasted_iota(jnp.int32, (n_heads * n, n), 1)
    mask = (col < row) & (col < N_META)
    heads = range(n_heads)
    _sb_step([q_ref[:, _lanes(h)] for h in heads], [k_ref[:, _lanes(h)] for h in heads],
             [v_ref[:, _lanes(h)] for h in heads], carry_ref, acc_ref, tri_ref[...][:n, :n], mask)
    for h in heads:
        o_ref[:, _lanes(h)] = acc_ref[h * n:(h + 1) * n, :].astype(o_ref.dtype)


def _attn_meta(q, kb, vb, tri, o_all, *, n_heads, meta_blk):
    w = n_heads * HEAD_DIM
    blk = pl.BlockSpec((ROW_BLOCK, w), lambda i: (meta_blk, 0))
    return pl.pallas_call(
        functools.partial(_attn_meta_kernel, n_heads=n_heads),
        grid=(1,),
        in_specs=[blk, blk, blk, pl.BlockSpec(tri.shape, lambda i: (0, 0)), pl.BlockSpec(memory_space=pl.ANY)],
        out_specs=blk,
        out_shape=jax.ShapeDtypeStruct(o_all.shape, o_all.dtype),
        input_output_aliases={4: 0},
        scratch_shapes=[pltpu.VMEM((n_heads * ROW_BLOCK, 1), F32), pltpu.VMEM((n_heads * ROW_BLOCK, HEAD_DIM), F32)],
        compiler_params=pltpu.CompilerParams(dimension_semantics=("arbitrary",)),
        name="attn_meta",
    )(q, kb, vb, tri, o_all)


def _attn_sample_kernel(q_ref, nk_ref, nv_ref, ck_ref, cv_ref, tri_ref, _o_in_ref, o_ref, carry_ref, acc_ref,
                        *, n_heads, tk, n_chunks):
    t = pl.program_id(1)
    lq = q_ref.shape[0]
    heads = range(n_heads)
    tri = tri_ref[...]
    qs = [q_ref[:, _lanes(h)] for h in heads]

    @pl.when(t == 0)
    def _():
        carry_ref[...] = jnp.zeros_like(carry_ref)
        acc_ref[...] = jnp.zeros_like(acc_ref)
        row = lax.broadcasted_iota(jnp.int32, (n_heads * lq, lq), 0) & (lq - 1)
        col = lax.broadcasted_iota(jnp.int32, (n_heads * lq, lq), 1)
        _sb_step(qs, [nk_ref[:, _lanes(h)] for h in heads], [nv_ref[:, _lanes(h)] for h in heads],
                 carry_ref, acc_ref, tri[:lq, :lq], col < row)

    n_sub = ck_ref.shape[0] // (tk * n_heads)
    def sub_block(s):
        head_rows = lambda h: pl.ds(s * tk * n_heads + h, tk, stride=n_heads)
        return ([ck_ref[head_rows(h), :].astype(BF16) for h in heads],
                [cv_ref[head_rows(h), :].astype(BF16) for h in heads])

    _sb_accumulate(carry_ref, acc_ref, _sb_blocks(qs, [sub_block(s) for s in reversed(range(n_sub))], tri, None))

    @pl.when(t == n_chunks - 1)
    def _():
        for h in heads:
            o_ref[:, _lanes(h)] = acc_ref[h * lq:(h + 1) * lq, :].astype(o_ref.dtype)


def _attn_sample(q, kb, vb, cache_k, cache_v, tri, o_all, *, layer, lq, row0, chunk):
    _, n_batch, past, n_heads, _ = cache_k.shape
    assert lq & (lq - 1) == 0
    tk = tri.shape[0]
    width = n_heads * HEAD_DIM
    n_chunks = past // chunk
    blk0 = row0 // lq
    new_blk = pl.BlockSpec((lq, width), lambda b, t: (blk0 + b, 0))
    cache_rows = lambda c: c.reshape(-1, HEAD_DIM)
    cache_blk = pl.BlockSpec((chunk * n_heads, HEAD_DIM),
                             lambda b, t: ((layer * n_batch + b) * n_chunks + n_chunks - 1 - t, 0))
    return pl.pallas_call(
        functools.partial(_attn_sample_kernel, n_heads=n_heads, tk=tk, n_chunks=n_chunks),
        grid=(n_batch, n_chunks),
        in_specs=[new_blk, new_blk, new_blk, cache_blk, cache_blk, pl.BlockSpec(tri.shape, lambda b, t: (0, 0)),
                  pl.BlockSpec(memory_space=pl.ANY)],
        out_specs=new_blk,
        out_shape=jax.ShapeDtypeStruct(o_all.shape, o_all.dtype),
        input_output_aliases={6: 0},
        scratch_shapes=[pltpu.VMEM((n_heads * lq, 1), F32), pltpu.VMEM((n_heads * lq, HEAD_DIM), F32)],
        compiler_params=pltpu.CompilerParams(
            dimension_semantics=("arbitrary", "arbitrary"), vmem_limit_bytes=V7X_VMEM_LIMIT),
        name="attn_sample",
    )(q, kb, vb, cache_rows(cache_k), cache_rows(cache_v), tri, o_all)


def _mixout_kernel(osb_ref, u_ref, halo_ref, h_ref, wp_ref, ps_ref, wo_ref, g_ref, o_ref, ext_ref, cat_ref,
                   *, meta_step, sb_width):
    n_seq, lt, pw = u_ref.shape
    cg = pw // len(POOL_WINDOWS)
    rows = n_seq * lt
    ext_ref[:, :HALO, :] = halo_ref[...]
    ext_ref[:, HALO:, :] = u_ref[...]
    cat_ref[:, :sb_width] = osb_ref[...]

    n_hist = jnp.where(pl.program_id(0) == meta_step, 0, POOL_HIST)
    n_before = (n_hist + lax.broadcasted_iota(jnp.int32, (1, lt, 1), 1)).astype(F32)

    for g, w in enumerate(POOL_WINDOWS):
        lanes = pl.ds(g * cg, cg)
        cur = ext_ref[:, pl.ds(HALO, lt), lanes]
        s = cur
        for back in range(1, w):
            s = s + ext_ref[:, pl.ds(HALO - back, lt), lanes]
        cnt = jnp.minimum(float(w), n_before + 1.0)
        dlt = (s / cnt - cur).reshape(rows, cg).astype(BF16)
        y = _dot(dlt, wp_ref[g]) * ps_ref[:, lanes]
        cat_ref[:, pl.ds(sb_width + g * cg, cg)] = y.astype(BF16)

    mixed = _dot(cat_ref[...], wo_ref[...])
    o_ref[...] = h_ref[...] + _rms(mixed, g_ref[...])


def _mixout(o_sb, u, halo, h, w_pool, pool_scale, w_out, g):
    m, d = h.shape
    pw = u.shape[1]
    sb_width = o_sb.shape[1]
    n_steps = m // ROW_BLOCK
    seq_per = ROW_BLOCK // SEQ_TILE
    u3 = u.reshape(m // SEQ_TILE, SEQ_TILE, pw)
    return pl.pallas_call(
        functools.partial(_mixout_kernel, meta_step=n_steps - 1, sb_width=sb_width),
        grid=(n_steps,),
        in_specs=[
            pl.BlockSpec((ROW_BLOCK, sb_width), lambda i: (i, 0)),
            pl.BlockSpec((seq_per, SEQ_TILE, pw), lambda i: (i, 0, 0)),
            pl.BlockSpec((seq_per, HALO, pw), lambda i: (i, 0, 0)),
            pl.BlockSpec((ROW_BLOCK, d), lambda i: (i, 0)),
            pl.BlockSpec(w_pool.shape, lambda i: (0, 0, 0)),
            pl.BlockSpec((1, pw), lambda i: (0, 0)),
            pl.BlockSpec(w_out.shape, lambda i: (0, 0)),
            pl.BlockSpec((1, d), lambda i: (0, 0)),
        ],
        out_specs=pl.BlockSpec((ROW_BLOCK, d), lambda i: (i, 0)),
        out_shape=jax.ShapeDtypeStruct((m, d), F32),
        scratch_shapes=[pltpu.VMEM((seq_per, HALO + SEQ_TILE, pw), F32),
                        pltpu.VMEM((ROW_BLOCK, sb_width + pw), BF16)],
        compiler_params=pltpu.CompilerParams(
            dimension_semantics=("arbitrary",), vmem_limit_bytes=V7X_VMEM_LIMIT),
        name="mixout",
    )(o_sb, u3, halo, h, w_pool, pool_scale, w_out, g)


def _row_tile(m, cap):
    blocks = m // ROW_BLOCK
    best = 1
    for dvs in range(1, blocks + 1):
        if blocks % dvs == 0 and dvs * ROW_BLOCK <= cap:
            best = dvs
    return best * ROW_BLOCK


def _col_tile(n, cap):
    best = HEAD_DIM
    for t in range(HEAD_DIM, cap + 1, HEAD_DIM):
        if n % t == 0:
            best = t
    return best


def kernel(x_prompt, x_sample, cache_k, cache_v, state_pool, meta_tokens, w_in, w_out, w_pool, pool_scale,
           norm_gains, ffn1_gate, ffn1_up, ffn1_down, ffn2_gate, ffn2_up, ffn2_down):
    n_b, seq, d = x_prompt.shape
    s_b, s_len, _ = x_sample.shape
    depth, _, past, n_heads, head_dim = cache_k.shape
    pw = state_pool.shape[-1]
    sbw = n_heads * head_dim
    assert head_dim == HEAD_DIM and meta_tokens.shape[0] == N_META and sbw == pw
    assert w_in.shape[-1] == 3 * sbw + pw and state_pool.shape[2] == POOL_HIST
    assert seq % SEQ_TILE == 0 and s_len == SEQ_TILE and seq >= POOL_HIST

    r_p = n_b * seq
    r_t = r_p + s_b * s_len
    assert r_t % ROW_BLOCK == 0
    m = r_t + ROW_BLOCK
    meta_blk = r_t // ROW_BLOCK

    tk = 256
    assert seq % tk == 0 and past % tk == 0
    chunk = 1024 if past % 1024 == 0 else tk
    hg = 4 if n_heads % 4 == 0 else 1
    tri = (lax.broadcasted_iota(jnp.int32, (tk, tk), 0) >= lax.broadcasted_iota(jnp.int32, (tk, tk), 1)).astype(BF16)

    x = jnp.concatenate([
        x_prompt.reshape(r_p, d), x_sample.reshape(s_b * s_len, d), meta_tokens.astype(F32),
        jnp.zeros((ROW_BLOCK - N_META, d), F32)], axis=0)

    tm_ffn = _row_tile(m, 1152)
    tf = _col_tile(ffn1_gate.shape[-1], 256)
    tm_proj = _row_tile(m, 576)
    n_tiles_p = seq // SEQ_TILE

    outs = [[] for _ in range(6)]
    for l in range(depth):
        gains = norm_gains[l].astype(F32)
        gain = lambda n: gains[n][None, :]
        bf = lambda w: w[l].astype(BF16)

        x = _ffn(x, gain(0), gain(1), ffn1_gate, ffn1_up, ffn1_down, layer=l, tm=tm_ffn, tf=tf)
        q, kb, vb, k, v, u = _inproj(x, gain(2), w_in, layer=l, tm=tm_proj, width=sbw)

        o_sb = _attn_prompt(q, kb, vb, tri, n_batch=n_b, seq=seq, n_heads=n_heads, meta_blk=meta_blk, hg=hg)
        o_sb = _attn_sample(q, kb, vb, cache_k, cache_v, tri, o_sb, layer=l, lq=s_len, row0=r_p, chunk=chunk)
        o_sb = _attn_meta(q, kb, vb, tri, o_sb, n_heads=n_heads, meta_blk=meta_blk)

        u_p = u[:r_p].reshape(n_b, n_tiles_p, SEQ_TILE, pw)
        u_meta = jnp.broadcast_to(u[r_t:r_t + N_META][None, None], (n_b, 1, HALO, pw))
        halo_p = jnp.concatenate([u_meta, u_p[:, :-1, SEQ_TILE - HALO:]], axis=1).reshape(n_b * n_tiles_p, HALO, pw)
        halo_s = jnp.pad(state_pool[l].astype(F32), ((0, 0), (HALO - POOL_HIST, 0), (0, 0)))
        halo = jnp.concatenate([halo_p, halo_s, jnp.zeros((ROW_BLOCK // SEQ_TILE, HALO, pw), F32)], axis=0)

        x = _mixout(o_sb, u, halo, x, bf(w_pool), pool_scale[l].astype(F32)[None, :], bf(w_out), gain(3))
        x = _ffn(x, gain(4), gain(5), ffn2_gate, ffn2_up, ffn2_down, layer=l, tm=tm_ffn, tf=tf)

        def with_meta(a):
            meta = jnp.broadcast_to(a[r_t:r_t + N_META][None], (n_b, N_META, n_heads, HEAD_DIM))
            return jnp.concatenate([meta, a[:r_p].reshape(n_b, seq, n_heads, HEAD_DIM)], axis=1)

        k = k.reshape(m, n_heads, HEAD_DIM)
        v = v.reshape(m, n_heads, HEAD_DIM)
        outs[0].append(with_meta(k))
        outs[1].append(with_meta(v))
        outs[2].append(u[:r_p].reshape(n_b, seq, pw)[:, seq - POOL_HIST:])
        outs[3].append(k[r_p:r_t].reshape(s_b, s_len, n_heads, HEAD_DIM))
        outs[4].append(v[r_p:r_t].reshape(s_b, s_len, n_heads, HEAD_DIM))
        outs[5].append(u[r_p:r_t].reshape(s_b, s_len, pw)[:, s_len - POOL_HIST:])

    y_prompt = x[:r_p].reshape(n_b, seq, d)
    y_sample = x[r_p:r_t].reshape(s_b, s_len, d)
    return (y_prompt, y_sample) + tuple(jnp.stack(o) for o in outs)
```

```python
import functools

import jax
import jax.numpy as jnp
from jax import lax
from jax.experimental import pallas as pl
from jax.experimental.pallas import tpu as pltpu

F32 = jnp.float32
BF16 = jnp.bfloat16

RMS_EPS = 1e-6
HEAD_DIM = 128
MXU_WIDTH = 256
N_META = 16
POOL_WINDOWS = (2, 4, 8, 16)
POOL_HIST = max(POOL_WINDOWS) - 1
HALO = 16
SEQ_TILE = 64
ROW_BLOCK = 128
V7X_VMEM_BYTES = 64 * 1024 * 1024
V7X_VMEM_LIMIT = V7X_VMEM_BYTES - 4 * 1024 * 1024


def _rms(x, g):
    ms = jnp.mean(x * x, axis=-1, keepdims=True)
    return x * lax.rsqrt(ms + RMS_EPS) * g


def _dot(a, b):
    return jnp.dot(a, b, preferred_element_type=F32)


def _ffn_kernel(x_ref, gi_ref, go_ref, wg_ref, wu_ref, wd_ref, o_ref, xn_ref, *, n_f):
    j = pl.program_id(1)

    @pl.when(j == 0)
    def _():
        xn_ref[...] = _rms(x_ref[...], gi_ref[...]).astype(BF16)
        o_ref[...] = jnp.zeros_like(o_ref)

    xn = xn_ref[...]
    tf = wg_ref.shape[1]
    hidden = []
    for c in range(0, tf, MXU_WIDTH):
        g = _dot(xn, wg_ref[:, c:c + MXU_WIDTH].astype(BF16))
        u = _dot(xn, wu_ref[:, c:c + MXU_WIDTH].astype(BF16))
        hidden.append((g * jax.nn.sigmoid(g) * u).astype(BF16))
    hidden = jnp.concatenate(hidden, axis=1)
    for c in range(0, o_ref.shape[1], MXU_WIDTH):
        o_ref[:, c:c + MXU_WIDTH] += _dot(hidden, wd_ref[:, c:c + MXU_WIDTH].astype(BF16))

    @pl.when(j == n_f - 1)
    def _():
        go = go_ref[...]

        def finish_rows(r, _):
            rows = pl.ds(pl.multiple_of(r * ROW_BLOCK, ROW_BLOCK), ROW_BLOCK)
            o_ref[rows, :] = x_ref[rows, :] + 0.5 * _rms(o_ref[rows, :], go)
            return 0

        lax.fori_loop(0, o_ref.shape[0] // ROW_BLOCK, finish_rows, 0)


def _ffn(x, gi, go, wg, wu, wd, *, layer, tm, tf):
    m, d = x.shape
    f = wg.shape[2]
    n_f = f // tf
    return pl.pallas_call(
        functools.partial(_ffn_kernel, n_f=n_f),
        grid=(m // tm, n_f),
        in_specs=[
            pl.BlockSpec((tm, d), lambda i, j: (i, 0), pipeline_mode=pl.Buffered(1)),
            pl.BlockSpec((1, d), lambda i, j: (0, 0)),
            pl.BlockSpec((1, d), lambda i, j: (0, 0)),
            pl.BlockSpec((None, d, tf), lambda i, j: (layer, 0, j)),
            pl.BlockSpec((None, d, tf), lambda i, j: (layer, 0, j)),
            pl.BlockSpec((None, tf, d), lambda i, j: (layer, j, 0)),
        ],
        out_specs=pl.BlockSpec((tm, d), lambda i, j: (i, 0), pipeline_mode=pl.Buffered(1)),
        out_shape=jax.ShapeDtypeStruct((m, d), F32),
        scratch_shapes=[pltpu.VMEM((tm, d), BF16)],
        compiler_params=pltpu.CompilerParams(
            dimension_semantics=("arbitrary", "arbitrary"), vmem_limit_bytes=V7X_VMEM_LIMIT),
        name="ffn",
    )(x, gi, go, wg, wu, wd)


def _inproj_kernel(x_ref, g_ref, w_ref, q_ref, kb_ref, vb_ref, k_ref, v_ref, u_ref, *, q_scale, n_heads):
    tm = x_ref.shape[0]
    width = q_ref.shape[1]
    xn = _rms(x_ref[...], g_ref[...]).astype(BF16)

    def proj(c):
        return _dot(xn, w_ref[:, c * width:(c + 1) * width])

    q_ref[...] = (proj(0) * q_scale).astype(BF16)
    for c, ref, bf_ref in ((1, k_ref, kb_ref), (2, v_ref, vb_ref)):
        p = proj(c)
        for h in range(n_heads):
            ref[pl.ds(h, tm, stride=n_heads), :] = p[:, _lanes(h)]
        bf_ref[...] = p.astype(BF16)
    u_ref[...] = proj(3)


def _inproj(x, g, w_in, *, tm, width):
    m, d = x.shape
    n_heads = width // HEAD_DIM
    row_blk = lambda i: (i, 0)
    wide = pl.BlockSpec((tm, width), row_blk)
    by_head = pl.BlockSpec((tm * n_heads, HEAD_DIM), row_blk)
    out_bf = jax.ShapeDtypeStruct((m, width), BF16)
    out_heads = jax.ShapeDtypeStruct((m * n_heads, HEAD_DIM), F32)
    return pl.pallas_call(
        functools.partial(_inproj_kernel, q_scale=HEAD_DIM ** -0.5, n_heads=n_heads),
        grid=(m // tm,),
        in_specs=[
            pl.BlockSpec((tm, d), row_blk),
            pl.BlockSpec((1, d), lambda i: (0, 0)),
            pl.BlockSpec(w_in.shape, lambda i: (0, 0), pipeline_mode=pl.Buffered(1)),
        ],
        out_specs=[wide, wide, wide, by_head, by_head, wide],
        out_shape=[out_bf, out_bf, out_bf, out_heads, out_heads, jax.ShapeDtypeStruct((m, width), F32)],
        compiler_params=pltpu.CompilerParams(
            dimension_semantics=("arbitrary",), vmem_limit_bytes=V7X_VMEM_LIMIT),
        name="inproj",
    )(x, g, w_in)


def _lanes(h):
    return slice(h * HEAD_DIM, (h + 1) * HEAD_DIM)


def _sb_block(qs, ks, vs, tri, mask):
    return _sb_blocks(qs, [(ks, vs)], tri, mask)[0]


def _sb_blocks(qs, kvs, tri, mask):
    tq = qs[0].shape[0]
    tri2 = jnp.concatenate([tri, tri], axis=0)
    zs = [jnp.concatenate(
        [lax.dot_general(q, k, (((1,), (1,)), ((), ())), preferred_element_type=F32) for q, k in zip(qs, ks)],
        axis=0) for ks, _ in kvs]
    suffixes = []
    for z in zs:
        sp = jnp.maximum(z, 0.0) + jnp.log(1.0 + jnp.exp(-jnp.abs(z)))
        if mask is not None:
            sp = jnp.where(mask, sp, 0.0)
        hi = sp.astype(BF16)
        lo = (sp - hi.astype(F32)).astype(BF16)
        suffixes.append(_dot(jnp.concatenate([hi, lo], axis=1), tri2))
    out = []
    for z, suffix, (_, vs) in zip(zs, suffixes, kvs):
        p = jnp.exp(z - suffix)
        if mask is not None:
            p = jnp.where(mask, p, 0.0)
        pb = p.astype(BF16)
        pv = jnp.concatenate([_dot(pb[h * tq:(h + 1) * tq], v) for h, v in enumerate(vs)], axis=0)
        out.append((pv, suffix[:, :1]))
    return out


def _sb_accumulate(carry_ref, acc_ref, blocks):
    carry = carry_ref[...]
    acc = acc_ref[...]
    for pv, row_sum in blocks:
        acc = acc + jnp.exp(-carry) * pv
        carry = carry + row_sum
    carry_ref[...] = carry
    acc_ref[...] = acc


def _sb_step(qs, ks, vs, carry_ref, acc_ref, tri, mask):
    _sb_accumulate(carry_ref, acc_ref, [_sb_block(qs, ks, vs, tri, mask)])


def _attn_prompt_kernel(q_ref, k_ref, v_ref, mk_ref, mv_ref, tri_ref, o_ref, carry_ref, acc_ref, *, t, hg):
    i = pl.program_id(2)
    tri = tri_ref[...]
    qs = [q_ref[:, _lanes(h)] for h in range(hg)]
    carry_ref[...] = jnp.zeros_like(carry_ref)
    acc_ref[...] = jnp.zeros_like(acc_ref)

    def kv(c):
        off = pl.multiple_of(c * t, t)
        return ([k_ref[pl.ds(off, t), _lanes(h)] for h in range(hg)],
                [v_ref[pl.ds(off, t), _lanes(h)] for h in range(hg)])

    row = lax.broadcasted_iota(jnp.int32, (hg * t, t), 0) & (t - 1)
    col = lax.broadcasted_iota(jnp.int32, (hg * t, t), 1)
    ks, vs = kv(i)
    _sb_step(qs, ks, vs, carry_ref, acc_ref, tri, col < row)

    def pair(s, _):
        _sb_accumulate(carry_ref, acc_ref,
                       _sb_blocks(qs, [kv(i - 1 - 2 * s), kv(i - 2 - 2 * s)], tri, None))
        return 0

    lax.fori_loop(0, i // 2, pair, 0)

    @pl.when(i % 2 == 1)
    def _():
        ks, vs = kv(0)
        _sb_step(qs, ks, vs, carry_ref, acc_ref, tri, None)

    n_mk = mk_ref.shape[0]
    mcol = lax.broadcasted_iota(jnp.int32, (hg * t, n_mk), 1)
    _sb_step(qs, [mk_ref[:, _lanes(h)] for h in range(hg)], [mv_ref[:, _lanes(h)] for h in range(hg)],
             carry_ref, acc_ref, tri[:n_mk, :n_mk], mcol < N_META)
    for h in range(hg):
        o_ref[:, _lanes(h)] = acc_ref[h * t:(h + 1) * t, :].astype(o_ref.dtype)


def _attn_prompt(q, kb, vb, tri, *, n_batch, seq, n_heads, meta_blk, hg):
    t = tri.shape[0]
    assert t & (t - 1) == 0 and seq % t == 0 and n_heads % hg == 0
    n_q = seq // t
    w = hg * HEAD_DIM
    return pl.pallas_call(
        functools.partial(_attn_prompt_kernel, t=t, hg=hg),
        grid=(n_batch, n_heads // hg, n_q),
        in_specs=[
            pl.BlockSpec((t, w), lambda b, g, i: (b * n_q + i, g)),
            pl.BlockSpec((seq, w), lambda b, g, i: (b, g)),
            pl.BlockSpec((seq, w), lambda b, g, i: (b, g)),
            pl.BlockSpec((ROW_BLOCK, w), lambda b, g, i: (meta_blk, g)),
            pl.BlockSpec((ROW_BLOCK, w), lambda b, g, i: (meta_blk, g)),
            pl.BlockSpec(tri.shape, lambda b, g, i: (0, 0)),
        ],
        out_specs=pl.BlockSpec((t, w), lambda b, g, i: (b * n_q + i, g)),
        out_shape=jax.ShapeDtypeStruct(q.shape, BF16),
        scratch_shapes=[pltpu.VMEM((hg * t, 1), F32), pltpu.VMEM((hg * t, HEAD_DIM), F32)],
        compiler_params=pltpu.CompilerParams(
            dimension_semantics=("arbitrary", "arbitrary", "arbitrary"), vmem_limit_bytes=V7X_VMEM_LIMIT),
        name="attn_prompt",
    )(q, kb, vb, kb, vb, tri)


def _attn_meta_kernel(q_ref, k_ref, v_ref, tri_ref, _o_in_ref, o_ref, carry_ref, acc_ref, *, n_heads):
    n = q_ref.shape[0]
    carry_ref[...] = jnp.zeros_like(carry_ref)
    acc_ref[...] = jnp.zeros_like(acc_ref)
    row = lax.broadcasted_iota(jnp.int32, (n_heads * n, n), 0) & (n - 1)
    col = lax.broadcasted_iota(jnp.int32, (n_heads * n, n), 1)
    mask = (col < row) & (col < N_META)
    heads = range(n_heads)
    _sb_step([q_ref[:, _lanes(h)] for h in heads], [k_ref[:, _lanes(h)] for h in heads],
             [v_ref[:, _lanes(h)] for h in heads], carry_ref, acc_ref, tri_ref[...][:n, :n], mask)
    for h in heads:
        o_ref[:, _lanes(h)] = acc_ref[h * n:(h + 1) * n, :].astype(o_ref.dtype)


def _attn_meta(q, kb, vb, tri, o_all, *, n_heads, meta_blk):
    w = n_heads * HEAD_DIM
    blk = pl.BlockSpec((ROW_BLOCK, w), lambda i: (meta_blk, 0))
    return pl.pallas_call(
        functools.partial(_attn_meta_kernel, n_heads=n_heads),
        grid=(1,),
        in_specs=[blk, blk, blk, pl.BlockSpec(tri.shape, lambda i: (0, 0)), pl.BlockSpec(memory_space=pl.ANY)],
        out_specs=blk,
        out_shape=jax.ShapeDtypeStruct(o_all.shape, o_all.dtype),
        input_output_aliases={4: 0},
        scratch_shapes=[pltpu.VMEM((n_heads * ROW_BLOCK, 1), F32), pltpu.VMEM((n_heads * ROW_BLOCK, HEAD_DIM), F32)],
        compiler_params=pltpu.CompilerParams(dimension_semantics=("arbitrary",)),
        name="attn_meta",
    )(q, kb, vb, tri, o_all)


def _attn_sample_kernel(q_ref, nk_ref, nv_ref, ck_ref, cv_ref, tri_ref, _o_in_ref, o_ref, carry_ref, acc_ref,
                        *, n_heads, tk, n_chunks):
    t = pl.program_id(1)
    lq = q_ref.shape[0]
    heads = range(n_heads)
    tri = tri_ref[...]
    qs = [q_ref[:, _lanes(h)] for h in heads]

    @pl.when(t == 0)
    def _():
        carry_ref[...] = jnp.zeros_like(carry_ref)
        acc_ref[...] = jnp.zeros_like(acc_ref)
        row = lax.broadcasted_iota(jnp.int32, (n_heads * lq, lq), 0) & (lq - 1)
        col = lax.broadcasted_iota(jnp.int32, (n_heads * lq, lq), 1)
        _sb_step(qs, [nk_ref[:, _lanes(h)] for h in heads], [nv_ref[:, _lanes(h)] for h in heads],
                 carry_ref, acc_ref, tri[:lq, :lq], col < row)

    n_sub = ck_ref.shape[0] // (tk * n_heads)
    def sub_block(s):
        head_rows = lambda h: pl.ds(s * tk * n_heads + h, tk, stride=n_heads)
        return ([ck_ref[head_rows(h), :].astype(BF16) for h in heads],
                [cv_ref[head_rows(h), :].astype(BF16) for h in heads])

    _sb_accumulate(carry_ref, acc_ref, _sb_blocks(qs, [sub_block(s) for s in reversed(range(n_sub))], tri, None))

    @pl.when(t == n_chunks - 1)
    def _():
        for h in heads:
            o_ref[:, _lanes(h)] = acc_ref[h * lq:(h + 1) * lq, :].astype(o_ref.dtype)


def _attn_sample(q, kb, vb, cache_k, cache_v, tri, o_all, *, layer, lq, row0, chunk):
    _, n_batch, past, n_heads, _ = cache_k.shape
    assert lq & (lq - 1) == 0
    tk = tri.shape[0]
    width = n_heads * HEAD_DIM
    n_chunks = past // chunk
    blk0 = row0 // lq
    new_blk = pl.BlockSpec((lq, width), lambda b, t: (blk0 + b, 0))
    cache_rows = lambda c: c.reshape(-1, HEAD_DIM)
    cache_blk = pl.BlockSpec((chunk * n_heads, HEAD_DIM),
                             lambda b, t: ((layer * n_batch + b) * n_chunks + n_chunks - 1 - t, 0))
    return pl.pallas_call(
        functools.partial(_attn_sample_kernel, n_heads=n_heads, tk=tk, n_chunks=n_chunks),
        grid=(n_batch, n_chunks),
        in_specs=[new_blk, new_blk, new_blk, cache_blk, cache_blk, pl.BlockSpec(tri.shape, lambda b, t: (0, 0)),
                  pl.BlockSpec(memory_space=pl.ANY)],
        out_specs=new_blk,
        out_shape=jax.ShapeDtypeStruct(o_all.shape, o_all.dtype),
        input_output_aliases={6: 0},
        scratch_shapes=[pltpu.VMEM((n_heads * lq, 1), F32), pltpu.VMEM((n_heads * lq, HEAD_DIM), F32)],
        compiler_params=pltpu.CompilerParams(
            dimension_semantics=("arbitrary", "arbitrary"), vmem_limit_bytes=V7X_VMEM_LIMIT),
        name="attn_sample",
    )(q, kb, vb, cache_rows(cache_k), cache_rows(cache_v), tri, o_all)


def _mixout_kernel(osb_ref, u_ref, halo_ref, h_ref, wp_ref, ps_ref, wo_ref, g_ref, o_ref, ext_ref, cat_ref,
                   *, meta_step, sb_width):
    n_seq, lt, pw = u_ref.shape
    cg = pw // len(POOL_WINDOWS)
    rows = n_seq * lt
    ext_ref[:, :HALO, :] = halo_ref[...]
    ext_ref[:, HALO:, :] = u_ref[...]
    cat_ref[:, :sb_width] = osb_ref[...]

    n_hist = jnp.where(pl.program_id(0) == meta_step, 0, POOL_HIST)
    n_before = (n_hist + lax.broadcasted_iota(jnp.int32, (1, lt, 1), 1)).astype(F32)

    for g, w in enumerate(POOL_WINDOWS):
        lanes = pl.ds(g * cg, cg)
        cur = ext_ref[:, pl.ds(HALO, lt), lanes]
        s = cur
        for back in range(1, w):
            s = s + ext_ref[:, pl.ds(HALO - back, lt), lanes]
        cnt = jnp.minimum(float(w), n_before + 1.0)
        dlt = (s / cnt - cur).reshape(rows, cg).astype(BF16)
        y = _dot(dlt, wp_ref[g]) * ps_ref[:, lanes]
        cat_ref[:, pl.ds(sb_width + g * cg, cg)] = y.astype(BF16)

    mixed = _dot(cat_ref[...], wo_ref[...])
    o_ref[...] = h_ref[...] + _rms(mixed, g_ref[...])


def _mixout(o_sb, u, halo, h, w_pool, pool_scale, w_out, g):
    m, d = h.shape
    pw = u.shape[1]
    sb_width = o_sb.shape[1]
    n_steps = m // ROW_BLOCK
    seq_per = ROW_BLOCK // SEQ_TILE
    u3 = u.reshape(m // SEQ_TILE, SEQ_TILE, pw)
    return pl.pallas_call(
        functools.partial(_mixout_kernel, meta_step=n_steps - 1, sb_width=sb_width),
        grid=(n_steps,),
        in_specs=[
            pl.BlockSpec((ROW_BLOCK, sb_width), lambda i: (i, 0)),
            pl.BlockSpec((seq_per, SEQ_TILE, pw), lambda i: (i, 0, 0)),
            pl.BlockSpec((seq_per, HALO, pw), lambda i: (i, 0, 0)),
            pl.BlockSpec((ROW_BLOCK, d), lambda i: (i, 0)),
            pl.BlockSpec(w_pool.shape, lambda i: (0, 0, 0)),
            pl.BlockSpec((1, pw), lambda i: (0, 0)),
            pl.BlockSpec(w_out.shape, lambda i: (0, 0)),
            pl.BlockSpec((1, d), lambda i: (0, 0)),
        ],
        out_specs=pl.BlockSpec((ROW_BLOCK, d), lambda i: (i, 0)),
        out_shape=jax.ShapeDtypeStruct((m, d), F32),
        scratch_shapes=[pltpu.VMEM((seq_per, HALO + SEQ_TILE, pw), F32),
                        pltpu.VMEM((ROW_BLOCK, sb_width + pw), BF16)],
        compiler_params=pltpu.CompilerParams(
            dimension_semantics=("arbitrary",), vmem_limit_bytes=V7X_VMEM_LIMIT),
        name="mixout",
    )(o_sb, u3, halo, h, w_pool, pool_scale, w_out, g)


def _row_tile(m, cap):
    blocks = m // ROW_BLOCK
    best = 1
    for dvs in range(1, blocks + 1):
        if blocks % dvs == 0 and dvs * ROW_BLOCK <= cap:
            best = dvs
    return best * ROW_BLOCK


def _col_tile(n, cap):
    best = HEAD_DIM
    for t in range(HEAD_DIM, cap + 1, HEAD_DIM):
        if n % t == 0:
            best = t
    return best


def kernel(x_prompt, x_sample, cache_k, cache_v, state_pool, meta_tokens, w_in, w_out, w_pool, pool_scale,
           norm_gains, ffn1_gate, ffn1_up, ffn1_down, ffn2_gate, ffn2_up, ffn2_down):
    n_b, seq, d = x_prompt.shape
    s_b, s_len, _ = x_sample.shape
    depth, _, past, n_heads, head_dim = cache_k.shape
    pw = state_pool.shape[-1]
    sbw = n_heads * head_dim
    assert head_dim == HEAD_DIM and meta_tokens.shape[0] == N_META and sbw == pw
    assert w_in.shape[-1] == 3 * sbw + pw and state_pool.shape[2] == POOL_HIST
    assert seq % SEQ_TILE == 0 and s_len == SEQ_TILE and seq >= POOL_HIST

    r_p = n_b * seq
    r_t = r_p + s_b * s_len
    assert r_t % ROW_BLOCK == 0
    m = r_t + ROW_BLOCK
    meta_blk = r_t // ROW_BLOCK

    tk = 256
    assert seq % tk == 0 and past % tk == 0
    chunk = 1024 if past % 1024 == 0 else tk
    hg = 4 if n_heads % 4 == 0 else 1
    tri = (lax.broadcasted_iota(jnp.int32, (tk, tk), 0) >= lax.broadcasted_iota(jnp.int32, (tk, tk), 1)).astype(BF16)

    x = jnp.concatenate([
        x_prompt.reshape(r_p, d), x_sample.reshape(s_b * s_len, d), meta_tokens.astype(F32),
        jnp.zeros((ROW_BLOCK - N_META, d), F32)], axis=0)

    tm_ffn = _row_tile(m, 1152)
    tf = _col_tile(ffn1_gate.shape[-1], 512)
    tm_proj = _row_tile(m, 576)
    n_tiles_p = seq // SEQ_TILE

    outs = [[] for _ in range(6)]
    for l in range(depth):
        gains = norm_gains[l].astype(F32)
        gain = lambda n: gains[n][None, :]
        bf = lambda w: w[l].astype(BF16)

        x = _ffn(x, gain(0), gain(1), ffn1_gate, ffn1_up, ffn1_down, layer=l, tm=tm_ffn, tf=tf)
        q, kb, vb, k, v, u = _inproj(x, gain(2), bf(w_in), tm=tm_proj, width=sbw)

        o_sb = _attn_prompt(q, kb, vb, tri, n_batch=n_b, seq=seq, n_heads=n_heads, meta_blk=meta_blk, hg=hg)
        o_sb = _attn_sample(q, kb, vb, cache_k, cache_v, tri, o_sb, layer=l, lq=s_len, row0=r_p, chunk=chunk)
        o_sb = _attn_meta(q, kb, vb, tri, o_sb, n_heads=n_heads, meta_blk=meta_blk)

        u_p = u[:r_p].reshape(n_b, n_tiles_p, SEQ_TILE, pw)
        u_meta = jnp.broadcast_to(u[r_t:r_t + N_META][None, None], (n_b, 1, HALO, pw))
        halo_p = jnp.concatenate([u_meta, u_p[:, :-1, SEQ_TILE - HALO:]], axis=1).reshape(n_b * n_tiles_p, HALO, pw)
        halo_s = jnp.pad(state_pool[l].astype(F32), ((0, 0), (HALO - POOL_HIST, 0), (0, 0)))
        halo = jnp.concatenate([halo_p, halo_s, jnp.zeros((ROW_BLOCK // SEQ_TILE, HALO, pw), F32)], axis=0)

        x = _mixout(o_sb, u, halo, x, bf(w_pool), pool_scale[l].astype(F32)[None, :], bf(w_out), gain(3))
        x = _ffn(x, gain(4), gain(5), ffn2_gate, ffn2_up, ffn2_down, layer=l, tm=tm_ffn, tf=tf)

        def with_meta(a):
            meta = jnp.broadcast_to(a[r_t:r_t + N_META][None], (n_b, N_META, n_heads, HEAD_DIM))
            return jnp.concatenate([meta, a[:r_p].reshape(n_b, seq, n_heads, HEAD_DIM)], axis=1)

        k = k.reshape(m, n_heads, HEAD_DIM)
        v = v.reshape(m, n_heads, HEAD_DIM)
        outs[0].append(with_meta(k))
        outs[1].append(with_meta(v))
        outs[2].append(u[:r_p].reshape(n_b, seq, pw)[:, seq - POOL_HIST:])
        outs[3].append(k[r_p:r_t].reshape(s_b, s_len, n_heads, HEAD_DIM))
        outs[4].append(v[r_p:r_t].reshape(s_b, s_len, n_heads, HEAD_DIM))
        outs[5].append(u[r_p:r_t].reshape(s_b, s_len, pw)[:, s_len - POOL_HIST:])

    y_prompt = x[:r_p].reshape(n_b, seq, d)
    y_sample = x[r_p:r_t].reshape(s_b, s_len, d)
    return (y_prompt, y_sample) + tuple(jnp.stack(o) for o in outs)
```

```python
import functools

import jax
import jax.numpy as jnp
from jax import lax
from jax.experimental import pallas as pl
from jax.experimental.pallas import tpu as pltpu

F32 = jnp.float32
BF16 = jnp.bfloat16

RMS_EPS = 1e-6
HEAD_DIM = 128
MXU_WIDTH = 256
N_META = 16
POOL_WINDOWS = (2, 4, 8, 16)
POOL_HIST = max(POOL_WINDOWS) - 1
HALO = 16
SEQ_TILE = 64
ROW_BLOCK = 128
V7X_VMEM_BYTES = 64 * 1024 * 1024
V7X_VMEM_LIMIT = V7X_VMEM_BYTES - 4 * 1024 * 1024


def _rms(x, g):
    ms = jnp.mean(x * x, axis=-1, keepdims=True)
    return x * lax.rsqrt(ms + RMS_EPS) * g


def _dot(a, b):
    return jnp.dot(a, b, preferred_element_type=F32)


def _ffn_kernel(x_ref, gi_ref, go_ref, wg_ref, wu_ref, wd_ref, o_ref, xn_ref, *, n_f):
    j = pl.program_id(1)

    @pl.when(j == 0)
    def _():
        xn_ref[...] = _rms(x_ref[...], gi_ref[...]).astype(BF16)
        o_ref[...] = jnp.zeros_like(o_ref)

    xn = xn_ref[...]
    tf = wg_ref.shape[1]
    hidden = []
    for c in range(0, tf, MXU_WIDTH):
        g = _dot(xn, wg_ref[:, c:c + MXU_WIDTH].astype(BF16))
        u = _dot(xn, wu_ref[:, c:c + MXU_WIDTH].astype(BF16))
        hidden.append((g * jax.nn.sigmoid(g) * u).astype(BF16))
    hidden = jnp.concatenate(hidden, axis=1)
    for c in range(0, o_ref.shape[1], MXU_WIDTH):
        o_ref[:, c:c + MXU_WIDTH] += _dot(hidden, wd_ref[:, c:c + MXU_WIDTH].astype(BF16))

    @pl.when(j == n_f - 1)
    def _():
        go = go_ref[...]

        def finish_rows(r, _):
            rows = pl.ds(pl.multiple_of(r * ROW_BLOCK, ROW_BLOCK), ROW_BLOCK)
            o_ref[rows, :] = x_ref[rows, :] + 0.5 * _rms(o_ref[rows, :], go)
            return 0

        lax.fori_loop(0, o_ref.shape[0] // ROW_BLOCK, finish_rows, 0)


def _ffn(x, gi, go, wg, wu, wd, *, layer, tm, tf):
    m, d = x.shape
    f = wg.shape[2]
    n_f = f // tf
    return pl.pallas_call(
        functools.partial(_ffn_kernel, n_f=n_f),
        grid=(m // tm, n_f),
        in_specs=[
            pl.BlockSpec((tm, d), lambda i, j: (i, 0), pipeline_mode=pl.Buffered(1)),
            pl.BlockSpec((1, d), lambda i, j: (0, 0)),
            pl.BlockSpec((1, d), lambda i, j: (0, 0)),
            pl.BlockSpec((None, d, tf), lambda i, j: (layer, 0, j)),
            pl.BlockSpec((None, d, tf), lambda i, j: (layer, 0, j)),
            pl.BlockSpec((None, tf, d), lambda i, j: (layer, j, 0)),
        ],
        out_specs=pl.BlockSpec((tm, d), lambda i, j: (i, 0), pipeline_mode=pl.Buffered(1)),
        out_shape=jax.ShapeDtypeStruct((m, d), F32),
        scratch_shapes=[pltpu.VMEM((tm, d), BF16)],
        compiler_params=pltpu.CompilerParams(
            dimension_semantics=("arbitrary", "arbitrary"), vmem_limit_bytes=V7X_VMEM_LIMIT),
        name="ffn",
    )(x, gi, go, wg, wu, wd)


def _inproj_kernel(x_ref, g_ref, w_ref, q_ref, kb_ref, vb_ref, k_ref, v_ref, u_ref, *, q_scale, n_heads):
    tm = x_ref.shape[0]
    width = q_ref.shape[1]
    xn = _rms(x_ref[...], g_ref[...]).astype(BF16)

    def proj(c):
        return _dot(xn, w_ref[:, c * width:(c + 1) * width])

    q_ref[...] = (proj(0) * q_scale).astype(BF16)
    for c, ref, bf_ref in ((1, k_ref, kb_ref), (2, v_ref, vb_ref)):
        p = proj(c)
        for h in range(n_heads):
            ref[pl.ds(h, tm, stride=n_heads), :] = p[:, _lanes(h)]
        bf_ref[...] = p.astype(BF16)
    u_ref[...] = proj(3)


def _inproj(x, g, w_in, *, layer, tm, width):
    m, d = x.shape
    n_heads = width // HEAD_DIM
    row_blk = lambda i: (i, 0)
    wide = pl.BlockSpec((tm, width), row_blk)
    by_head = pl.BlockSpec((tm * n_heads, HEAD_DIM), row_blk)
    out_bf = jax.ShapeDtypeStruct((m, width), BF16)
    out_heads = jax.ShapeDtypeStruct((m * n_heads, HEAD_DIM), F32)
    return pl.pallas_call(
        functools.partial(_inproj_kernel, q_scale=HEAD_DIM ** -0.5, n_heads=n_heads),
        grid=(m // tm,),
        in_specs=[
            pl.BlockSpec((tm, d), row_blk),
            pl.BlockSpec((1, d), lambda i: (0, 0)),
            pl.BlockSpec((None,) + w_in.shape[1:], lambda i: (layer, 0, 0), pipeline_mode=pl.Buffered(1)),
        ],
        out_specs=[wide, wide, wide, by_head, by_head, wide],
        out_shape=[out_bf, out_bf, out_bf, out_heads, out_heads, jax.ShapeDtypeStruct((m, width), F32)],
        compiler_params=pltpu.CompilerParams(
            dimension_semantics=("arbitrary",), vmem_limit_bytes=V7X_VMEM_LIMIT),
        name="inproj",
    )(x, g, w_in)


def _lanes(h):
    return slice(h * HEAD_DIM, (h + 1) * HEAD_DIM)


def _sb_block(qs, ks, vs, tri, mask):
    return _sb_blocks(qs, [(ks, vs, mask)], tri)[0]


def _sb_blocks(qs, blocks, tri):
    tq = qs[0].shape[0]
    zs = [jnp.concatenate(
        [lax.dot_general(q, k, (((1,), (1,)), ((), ())), preferred_element_type=F32) for q, k in zip(qs, ks)],
        axis=0) for ks, _, _ in blocks]
    suffixes = []
    for z, (_, _, mask) in zip(zs, blocks):
        sp = jnp.maximum(z, 0.0) + jnp.log(1.0 + jnp.exp(-jnp.abs(z)))
        if mask is not None:
            sp = jnp.where(mask, sp, 0.0)
        hi = sp.astype(BF16)
        lo = (sp - hi.astype(F32)).astype(BF16)
        tk = z.shape[1]
        tri2 = jnp.concatenate([tri[:tk, :tk], tri[:tk, :tk]], axis=0)
        suffixes.append(_dot(jnp.concatenate([hi, lo], axis=1), tri2))
    out = []
    for z, suffix, (_, vs, mask) in zip(zs, suffixes, blocks):
        p = jnp.exp(z - suffix)
        if mask is not None:
            p = jnp.where(mask, p, 0.0)
        pb = p.astype(BF16)
        pv = jnp.concatenate([_dot(pb[h * tq:(h + 1) * tq], v) for h, v in enumerate(vs)], axis=0)
        out.append((pv, suffix[:, :1]))
    return out


def _sb_accumulate(carry_ref, acc_ref, blocks):
    carry = carry_ref[...]
    acc = acc_ref[...]
    for pv, row_sum in blocks:
        acc = acc + jnp.exp(-carry) * pv
        carry = carry + row_sum
    carry_ref[...] = carry
    acc_ref[...] = acc


def _sb_step(qs, ks, vs, carry_ref, acc_ref, tri, mask):
    _sb_accumulate(carry_ref, acc_ref, [_sb_block(qs, ks, vs, tri, mask)])


def _attn_prompt_kernel(q_ref, k_ref, v_ref, mk_ref, mv_ref, tri_ref, _o_in_ref, o_ref, carry_ref, acc_ref,
                        meta_pv_ref, meta_sum_ref, *, t, hg):
    i = pl.program_id(2)
    tri = tri_ref[...]
    heads = range(hg)
    qs = [q_ref[:, _lanes(h)] for h in heads]
    carry_ref[...] = jnp.zeros_like(carry_ref)
    acc_ref[...] = jnp.zeros_like(acc_ref)

    def token_block(c, mask=None):
        off = pl.multiple_of(c * t, t)
        return ([k_ref[pl.ds(off, t), _lanes(h)] for h in heads],
                [v_ref[pl.ds(off, t), _lanes(h)] for h in heads], mask)

    row = lax.broadcasted_iota(jnp.int32, (hg * t, t), 0) & (t - 1)
    col = lax.broadcasted_iota(jnp.int32, (hg * t, t), 1)
    mcol = lax.broadcasted_iota(jnp.int32, (hg * t, mk_ref.shape[0]), 1)
    diag = token_block(i, col < row)
    meta = ([mk_ref[:, _lanes(h)] for h in heads], [mv_ref[:, _lanes(h)] for h in heads], mcol < N_META)

    def head_group(token_blocks):
        *done, (meta_pv, meta_sum) = _sb_blocks(qs, token_blocks + [meta], tri)
        _sb_accumulate(carry_ref, acc_ref, done)
        meta_pv_ref[...] = meta_pv
        meta_sum_ref[...] = meta_sum

    @pl.when(i % 2 == 1)
    def _():
        head_group([diag, token_block(i - 1)])

    @pl.when(i % 2 == 0)
    def _():
        head_group([diag])

    top = i - i % 2

    def pair(s, _):
        _sb_accumulate(carry_ref, acc_ref,
                       _sb_blocks(qs, [token_block(top - 1 - 2 * s), token_block(top - 2 - 2 * s)], tri))
        return 0

    lax.fori_loop(0, top // 2, pair, 0)
    _sb_accumulate(carry_ref, acc_ref, [(meta_pv_ref[...], meta_sum_ref[...])])
    for h in heads:
        o_ref[:, _lanes(h)] = acc_ref[h * t:(h + 1) * t, :].astype(o_ref.dtype)


def _attn_prompt(q, kb, vb, tri, o_all, *, n_batch, seq, n_heads, meta_blk, hg):
    t = tri.shape[0]
    assert t & (t - 1) == 0 and seq % t == 0 and n_heads % hg == 0
    n_q = seq // t
    w = hg * HEAD_DIM
    return pl.pallas_call(
        functools.partial(_attn_prompt_kernel, t=t, hg=hg),
        grid=(n_batch, n_heads // hg, n_q),
        in_specs=[
            pl.BlockSpec((t, w), lambda b, g, i: (b * n_q + i, g)),
            pl.BlockSpec((seq, w), lambda b, g, i: (b, g)),
            pl.BlockSpec((seq, w), lambda b, g, i: (b, g)),
            pl.BlockSpec((ROW_BLOCK, w), lambda b, g, i: (meta_blk, g)),
            pl.BlockSpec((ROW_BLOCK, w), lambda b, g, i: (meta_blk, g)),
            pl.BlockSpec(tri.shape, lambda b, g, i: (0, 0)),
            pl.BlockSpec(memory_space=pl.ANY),
        ],
        out_specs=pl.BlockSpec((t, w), lambda b, g, i: (b * n_q + i, g)),
        out_shape=jax.ShapeDtypeStruct(o_all.shape, o_all.dtype),
        input_output_aliases={6: 0},
        scratch_shapes=[pltpu.VMEM((hg * t, 1), F32), pltpu.VMEM((hg * t, HEAD_DIM), F32),
                        pltpu.VMEM((hg * t, HEAD_DIM), F32), pltpu.VMEM((hg * t, 1), F32)],
        compiler_params=pltpu.CompilerParams(
            dimension_semantics=("arbitrary", "arbitrary", "arbitrary"), vmem_limit_bytes=V7X_VMEM_LIMIT),
        name="attn_prompt",
    )(q, kb, vb, kb, vb, tri, o_all)


def _attn_meta_kernel(q_ref, k_ref, v_ref, tri_ref, _o_in_ref, o_ref, carry_ref, acc_ref, *, n_heads):
    n = q_ref.shape[0]
    carry_ref[...] = jnp.zeros_like(carry_ref)
    acc_ref[...] = jnp.zeros_like(acc_ref)
    row = lax.broadcasted_iota(jnp.int32, (n_heads * n, n), 0) & (n - 1)
    col = lax.broadcasted_iota(jnp.int32, (n_heads * n, n), 1)
    mask = (col < row) & (col < N_META)
    heads = range(n_heads)
    _sb_step([q_ref[:, _lanes(h)] for h in heads], [k_ref[:, _lanes(h)] for h in heads],
             [v_ref[:, _lanes(h)] for h in heads], carry_ref, acc_ref, tri_ref[...], mask)
    for h in heads:
        o_ref[:, _lanes(h)] = acc_ref[h * n:(h + 1) * n, :].astype(o_ref.dtype)


def _attn_meta(q, kb, vb, tri, o_all, *, n_heads, meta_blk):
    w = n_heads * HEAD_DIM
    blk = pl.BlockSpec((ROW_BLOCK, w), lambda i: (meta_blk, 0))
    return pl.pallas_call(
        functools.partial(_attn_meta_kernel, n_heads=n_heads),
        grid=(1,),
        in_specs=[blk, blk, blk, pl.BlockSpec(tri.shape, lambda i: (0, 0)), pl.BlockSpec(memory_space=pl.ANY)],
        out_specs=blk,
        out_shape=jax.ShapeDtypeStruct(o_all.shape, o_all.dtype),
        input_output_aliases={4: 0},
        scratch_shapes=[pltpu.VMEM((n_heads * ROW_BLOCK, 1), F32), pltpu.VMEM((n_heads * ROW_BLOCK, HEAD_DIM), F32)],
        compiler_params=pltpu.CompilerParams(dimension_semantics=("arbitrary",)),
        name="attn_meta",
    )(q, kb, vb, tri, o_all)


def _attn_sample_kernel(q_ref, nk_ref, nv_ref, ck_ref, cv_ref, tri_ref, _o_in_ref, o_ref, carry_ref, acc_ref,
                        *, n_heads, tk, n_chunks):
    t = pl.program_id(1)
    lq = q_ref.shape[0]
    heads = range(n_heads)
    tri = tri_ref[...]
    qs = [q_ref[:, _lanes(h)] for h in heads]

    @pl.when(t == 0)
    def _():
        carry_ref[...] = jnp.zeros_like(carry_ref)
        acc_ref[...] = jnp.zeros_like(acc_ref)
        row = lax.broadcasted_iota(jnp.int32, (n_heads * lq, lq), 0) & (lq - 1)
        col = lax.broadcasted_iota(jnp.int32, (n_heads * lq, lq), 1)
        _sb_step(qs, [nk_ref[:, _lanes(h)] for h in heads], [nv_ref[:, _lanes(h)] for h in heads],
                 carry_ref, acc_ref, tri, col < row)

    n_sub = ck_ref.shape[0] // (tk * n_heads)

    def sub_block(s):
        head_rows = lambda h: pl.ds(s * tk * n_heads + h, tk, stride=n_heads)
        return ([ck_ref[head_rows(h), :].astype(BF16) for h in heads],
                [cv_ref[head_rows(h), :].astype(BF16) for h in heads], None)

    _sb_accumulate(carry_ref, acc_ref, _sb_blocks(qs, [sub_block(s) for s in reversed(range(n_sub))], tri))

    @pl.when(t == n_chunks - 1)
    def _():
        for h in heads:
            o_ref[:, _lanes(h)] = acc_ref[h * lq:(h + 1) * lq, :].astype(o_ref.dtype)


def _attn_sample(q, kb, vb, cache_k, cache_v, tri, o_all, *, layer, lq, row0, chunk):
    _, n_batch, past, n_heads, _ = cache_k.shape
    assert lq & (lq - 1) == 0
    tk = tri.shape[0]
    width = n_heads * HEAD_DIM
    n_chunks = past // chunk
    blk0 = row0 // lq
    new_blk = pl.BlockSpec((lq, width), lambda b, t: (blk0 + b, 0))
    cache_rows = lambda c: c.reshape(-1, HEAD_DIM)
    cache_blk = pl.BlockSpec((chunk * n_heads, HEAD_DIM),
                             lambda b, t: ((layer * n_batch + b) * n_chunks + n_chunks - 1 - t, 0))
    return pl.pallas_call(
        functools.partial(_attn_sample_kernel, n_heads=n_heads, tk=tk, n_chunks=n_chunks),
        grid=(n_batch, n_chunks),
        in_specs=[new_blk, new_blk, new_blk, cache_blk, cache_blk, pl.BlockSpec(tri.shape, lambda b, t: (0, 0)),
                  pl.BlockSpec(memory_space=pl.ANY)],
        out_specs=new_blk,
        out_shape=jax.ShapeDtypeStruct(o_all.shape, o_all.dtype),
        input_output_aliases={6: 0},
        scratch_shapes=[pltpu.VMEM((n_heads * lq, 1), F32), pltpu.VMEM((n_heads * lq, HEAD_DIM), F32)],
        compiler_params=pltpu.CompilerParams(
            dimension_semantics=("arbitrary", "arbitrary"), vmem_limit_bytes=V7X_VMEM_LIMIT),
        name="attn_sample",
    )(q, kb, vb, cache_rows(cache_k), cache_rows(cache_v), tri, o_all)


def _mixout_kernel(osb_ref, u_ref, halo_ref, h_ref, wp_ref, ps_ref, wo_ref, g_ref, o_ref, ext_ref, cat_ref,
                   *, meta_seq, sb_width):
    n_seq, lt, pw = u_ref.shape
    cg = pw // len(POOL_WINDOWS)
    rows = n_seq * lt
    ext_ref[:, :HALO, :] = halo_ref[...]
    ext_ref[:, HALO:, :] = u_ref[...]
    cat_ref[:, :sb_width] = osb_ref[...]

    seq_id = pl.program_id(0) * n_seq + lax.broadcasted_iota(jnp.int32, (n_seq, 1, 1), 0)
    n_hist = jnp.where(seq_id >= meta_seq, 0, POOL_HIST)
    n_before = (n_hist + lax.broadcasted_iota(jnp.int32, (n_seq, lt, 1), 1)).astype(F32)

    for g, w in enumerate(POOL_WINDOWS):
        lanes = pl.ds(g * cg, cg)
        cur = ext_ref[:, pl.ds(HALO, lt), lanes]
        s = cur
        for back in range(1, w):
            s = s + ext_ref[:, pl.ds(HALO - back, lt), lanes]
        cnt = jnp.minimum(float(w), n_before + 1.0)
        dlt = (s / cnt - cur).reshape(rows, cg).astype(BF16)
        y = _dot(dlt, wp_ref[g]) * ps_ref[:, lanes]
        cat_ref[:, pl.ds(sb_width + g * cg, cg)] = y.astype(BF16)

    mixed = _dot(cat_ref[...], wo_ref[...])
    o_ref[...] = h_ref[...] + _rms(mixed, g_ref[...])


def _mixout(o_sb, u, halo, h, w_pool, pool_scale, w_out, g, *, layer, tm):
    m, d = h.shape
    pw = u.shape[1]
    sb_width = o_sb.shape[1]
    seq_per = tm // SEQ_TILE
    u3 = u.reshape(m // SEQ_TILE, SEQ_TILE, pw)
    return pl.pallas_call(
        functools.partial(_mixout_kernel, meta_seq=(m - ROW_BLOCK) // SEQ_TILE, sb_width=sb_width),
        grid=(m // tm,),
        in_specs=[
            pl.BlockSpec((tm, sb_width), lambda i: (i, 0)),
            pl.BlockSpec((seq_per, SEQ_TILE, pw), lambda i: (i, 0, 0)),
            pl.BlockSpec((seq_per, HALO, pw), lambda i: (i, 0, 0)),
            pl.BlockSpec((tm, d), lambda i: (i, 0)),
            pl.BlockSpec((None,) + w_pool.shape[1:], lambda i: (layer, 0, 0, 0)),
            pl.BlockSpec((1, pw), lambda i: (0, 0)),
            pl.BlockSpec((None,) + w_out.shape[1:], lambda i: (layer, 0, 0)),
            pl.BlockSpec((1, d), lambda i: (0, 0)),
        ],
        out_specs=pl.BlockSpec((tm, d), lambda i: (i, 0)),
        out_shape=jax.ShapeDtypeStruct((m, d), F32),
        scratch_shapes=[pltpu.VMEM((seq_per, HALO + SEQ_TILE, pw), F32),
                        pltpu.VMEM((tm, sb_width + pw), BF16)],
        compiler_params=pltpu.CompilerParams(
            dimension_semantics=("arbitrary",), vmem_limit_bytes=V7X_VMEM_LIMIT),
        name="mixout",
    )(o_sb, u3, halo, h, w_pool, pool_scale, w_out, g)


def _row_tile(m, cap):
    blocks = m // ROW_BLOCK
    best = 1
    for dvs in range(1, blocks + 1):
        if blocks % dvs == 0 and dvs * ROW_BLOCK <= cap:
            best = dvs
    return best * ROW_BLOCK


def _col_tile(n, cap):
    best = HEAD_DIM
    for t in range(HEAD_DIM, cap + 1, HEAD_DIM):
        if n % t == 0:
            best = t
    return best


def kernel(x_prompt, x_sample, cache_k, cache_v, state_pool, meta_tokens, w_in, w_out, w_pool, pool_scale,
           norm_gains, ffn1_gate, ffn1_up, ffn1_down, ffn2_gate, ffn2_up, ffn2_down):
    n_b, seq, d = x_prompt.shape
    s_b, s_len, _ = x_sample.shape
    depth, _, past, n_heads, head_dim = cache_k.shape
    pw = state_pool.shape[-1]
    sbw = n_heads * head_dim
    assert head_dim == HEAD_DIM and meta_tokens.shape[0] == N_META and sbw == pw
    assert w_in.shape[-1] == 3 * sbw + pw and state_pool.shape[2] == POOL_HIST
    assert seq % SEQ_TILE == 0 and s_len == SEQ_TILE and seq >= POOL_HIST and N_META == HALO

    r_p = n_b * seq
    r_t = r_p + s_b * s_len
    assert r_t % ROW_BLOCK == 0
    m = r_t + ROW_BLOCK
    meta_blk = r_t // ROW_BLOCK

    tk = 256
    assert seq % tk == 0 and past % tk == 0
    chunk = 1024 if past % 1024 == 0 else tk
    hg = 4 if n_heads % 4 == 0 else 1
    tri = (lax.broadcasted_iota(jnp.int32, (tk, tk), 0) >= lax.broadcasted_iota(jnp.int32, (tk, tk), 1)).astype(BF16)

    x = jnp.concatenate([
        x_prompt.reshape(r_p, d), x_sample.reshape(s_b * s_len, d), meta_tokens.astype(F32),
        jnp.zeros((ROW_BLOCK - N_META, d), F32)], axis=0)

    tm_ffn = _row_tile(m, 1152)
    tf = _col_tile(ffn1_gate.shape[-1], 512)
    tm_proj = _row_tile(m, 576)
    n_tiles_p = seq // SEQ_TILE

    w_in_bf, w_out_bf, w_pool_bf = (w.astype(BF16) for w in (w_in, w_out, w_pool))
    seq_of = lambda row: row // SEQ_TILE
    outs = [[] for _ in range(6)]
    for l in range(depth):
        gains = norm_gains[l].astype(F32)
        gain = lambda n: gains[n][None, :]

        x = _ffn(x, gain(0), gain(1), ffn1_gate, ffn1_up, ffn1_down, layer=l, tm=tm_ffn, tf=tf)
        q, kb, vb, k, v, u = _inproj(x, gain(2), w_in_bf, layer=l, tm=tm_proj, width=sbw)

        o_sb = jnp.zeros((m, sbw), BF16)
        o_sb = _attn_prompt(q, kb, vb, tri, o_sb, n_batch=n_b, seq=seq, n_heads=n_heads, meta_blk=meta_blk, hg=hg)
        o_sb = _attn_sample(q, kb, vb, cache_k, cache_v, tri, o_sb, layer=l, lq=s_len, row0=r_p, chunk=chunk)
        o_sb = _attn_meta(q, kb, vb, tri, o_sb, n_heads=n_heads, meta_blk=meta_blk)

        tails = u.reshape(seq_of(m), SEQ_TILE, pw)[:, SEQ_TILE - HALO:]
        halo_p = jnp.concatenate(
            [part for b in range(n_b)
             for part in (u[r_t:r_t + N_META][None], tails[b * n_tiles_p:(b + 1) * n_tiles_p - 1])], axis=0)
        halo_s = jnp.pad(state_pool[l].astype(F32), ((0, 0), (HALO - POOL_HIST, 0), (0, 0)))
        halo = jnp.concatenate([halo_p, halo_s, jnp.zeros((ROW_BLOCK // SEQ_TILE, HALO, pw), F32)], axis=0)

        x = _mixout(o_sb, u, halo, x, w_pool_bf, pool_scale[l].astype(F32)[None, :], w_out_bf, gain(3),
                    layer=l, tm=tm_proj)
        x = _ffn(x, gain(4), gain(5), ffn2_gate, ffn2_up, ffn2_down, layer=l, tm=tm_ffn, tf=tf)

        k = k.reshape(m, n_heads, HEAD_DIM)
        v = v.reshape(m, n_heads, HEAD_DIM)
        for a, pieces in ((k, outs[0]), (v, outs[1])):
            for b in range(n_b):
                pieces += [a[r_t:r_t + N_META], a[b * seq:(b + 1) * seq]]
        outs[2].append(u[:r_p].reshape(n_b, seq, pw)[:, seq - POOL_HIST:])
        outs[3].append(k[r_p:r_t].reshape(s_b, s_len, n_heads, HEAD_DIM))
        outs[4].append(v[r_p:r_t].reshape(s_b, s_len, n_heads, HEAD_DIM))
        outs[5].append(u[r_p:r_t].reshape(s_b, s_len, pw)[:, s_len - POOL_HIST:])

    y_prompt = x[:r_p].reshape(n_b, seq, d)
    y_sample = x[r_p:r_t].reshape(s_b, s_len, d)
    prompt_cache = lambda pieces: jnp.concatenate(pieces, axis=0).reshape(
        depth, n_b, N_META + seq, n_heads, HEAD_DIM)
    return (y_prompt, y_sample, prompt_cache(outs[0]), prompt_cache(outs[1])) + tuple(
        jnp.stack(o) for o in outs[2:])
```

```python
import functools

import jax
import jax.numpy as jnp
from jax import lax
from jax.experimental import pallas as pl
from jax.experimental.pallas import tpu as pltpu

F32 = jnp.float32
BF16 = jnp.bfloat16

RMS_EPS = 1e-6
HEAD_DIM = 128
MXU_WIDTH = 256
N_META = 16
POOL_WINDOWS = (2, 4, 8, 16)
POOL_HIST = max(POOL_WINDOWS) - 1
HALO = 16
SEQ_TILE = 64
ROW_BLOCK = 128
LOOP_BLOCKS = 2
V7X_VMEM_BYTES = 64 * 1024 * 1024
V7X_VMEM_LIMIT = V7X_VMEM_BYTES - 4 * 1024 * 1024


def _rms(x, g):
    ms = jnp.mean(x * x, axis=-1, keepdims=True)
    return x * lax.rsqrt(ms + RMS_EPS) * g


def _dot(a, b):
    return jnp.dot(a, b, preferred_element_type=F32)


def _ffn_kernel(x_ref, gi_ref, go_ref, wg_ref, wu_ref, wd_ref, o_ref, xn_ref, *, n_f):
    j = pl.program_id(1)

    @pl.when(j == 0)
    def _():
        xn_ref[...] = _rms(x_ref[...], gi_ref[...]).astype(BF16)
        o_ref[...] = jnp.zeros_like(o_ref)

    xn = xn_ref[...]
    tf = wg_ref.shape[1]
    hidden = []
    for c in range(0, tf, MXU_WIDTH):
        g = _dot(xn, wg_ref[:, c:c + MXU_WIDTH].astype(BF16))
        u = _dot(xn, wu_ref[:, c:c + MXU_WIDTH].astype(BF16))
        hidden.append((g * jax.nn.sigmoid(g) * u).astype(BF16))
    hidden = jnp.concatenate(hidden, axis=1)
    for c in range(0, o_ref.shape[1], MXU_WIDTH):
        o_ref[:, c:c + MXU_WIDTH] += _dot(hidden, wd_ref[:, c:c + MXU_WIDTH].astype(BF16))

    @pl.when(j == n_f - 1)
    def _():
        go = go_ref[...]

        def finish_rows(r, _):
            rows = pl.ds(pl.multiple_of(r * ROW_BLOCK, ROW_BLOCK), ROW_BLOCK)
            o_ref[rows, :] = x_ref[rows, :] + 0.5 * _rms(o_ref[rows, :], go)
            return 0

        lax.fori_loop(0, o_ref.shape[0] // ROW_BLOCK, finish_rows, 0)


def _ffn(x, gi, go, wg, wu, wd, *, layer, tm, tf):
    m, d = x.shape
    f = wg.shape[2]
    n_f = f // tf
    return pl.pallas_call(
        functools.partial(_ffn_kernel, n_f=n_f),
        grid=(m // tm, n_f),
        in_specs=[
            pl.BlockSpec((tm, d), lambda i, j: (i, 0), pipeline_mode=pl.Buffered(1)),
            pl.BlockSpec((1, d), lambda i, j: (0, 0)),
            pl.BlockSpec((1, d), lambda i, j: (0, 0)),
            pl.BlockSpec((None, d, tf), lambda i, j: (layer, 0, j)),
            pl.BlockSpec((None, d, tf), lambda i, j: (layer, 0, j)),
            pl.BlockSpec((None, tf, d), lambda i, j: (layer, j, 0)),
        ],
        out_specs=pl.BlockSpec((tm, d), lambda i, j: (i, 0), pipeline_mode=pl.Buffered(1)),
        out_shape=jax.ShapeDtypeStruct((m, d), F32),
        scratch_shapes=[pltpu.VMEM((tm, d), BF16)],
        compiler_params=pltpu.CompilerParams(
            dimension_semantics=("arbitrary", "arbitrary"), vmem_limit_bytes=V7X_VMEM_LIMIT),
        name="ffn",
    )(x, gi, go, wg, wu, wd)


def _inproj_kernel(x_ref, g_ref, w_ref, q_ref, kb_ref, vb_ref, k_ref, v_ref, u_ref, *, q_scale, n_heads):
    tm = x_ref.shape[0]
    width = q_ref.shape[1]
    xn = _rms(x_ref[...], g_ref[...]).astype(BF16)

    def proj(c):
        return _dot(xn, w_ref[:, c * width:(c + 1) * width])

    q_ref[...] = (proj(0) * q_scale).astype(BF16)
    for c, ref, bf_ref in ((1, k_ref, kb_ref), (2, v_ref, vb_ref)):
        p = proj(c)
        for h in range(n_heads):
            ref[pl.ds(h, tm, stride=n_heads), :] = p[:, _lanes(h)]
        bf_ref[...] = p.astype(BF16)
    u_ref[...] = proj(3)


def _inproj(x, g, w_in, *, layer, tm, width):
    m, d = x.shape
    n_heads = width // HEAD_DIM
    row_blk = lambda i: (i, 0)
    wide = pl.BlockSpec((tm, width), row_blk)
    by_head = pl.BlockSpec((tm * n_heads, HEAD_DIM), row_blk)
    out_bf = jax.ShapeDtypeStruct((m, width), BF16)
    out_heads = jax.ShapeDtypeStruct((m * n_heads, HEAD_DIM), F32)
    return pl.pallas_call(
        functools.partial(_inproj_kernel, q_scale=HEAD_DIM ** -0.5, n_heads=n_heads),
        grid=(m // tm,),
        in_specs=[
            pl.BlockSpec((tm, d), row_blk),
            pl.BlockSpec((1, d), lambda i: (0, 0)),
            pl.BlockSpec((None,) + w_in.shape[1:], lambda i: (layer, 0, 0), pipeline_mode=pl.Buffered(1)),
        ],
        out_specs=[wide, wide, wide, by_head, by_head, wide],
        out_shape=[out_bf, out_bf, out_bf, out_heads, out_heads, jax.ShapeDtypeStruct((m, width), F32)],
        compiler_params=pltpu.CompilerParams(
            dimension_semantics=("arbitrary",), vmem_limit_bytes=V7X_VMEM_LIMIT),
        name="inproj",
    )(x, g, w_in)


def _lanes(h):
    return slice(h * HEAD_DIM, (h + 1) * HEAD_DIM)


def _sb_block(qs, ks, vs, tri, mask):
    return _sb_blocks(qs, [(ks, vs, mask)], tri)[0]


def _sb_blocks(qs, blocks, tri):
    tq = qs[0].shape[0]
    zs = [jnp.concatenate(
        [lax.dot_general(q, k, (((1,), (1,)), ((), ())), preferred_element_type=F32) for q, k in zip(qs, ks)],
        axis=0) for ks, _, _ in blocks]
    suffixes = []
    for z, (_, _, mask) in zip(zs, blocks):
        sp = jnp.maximum(z, 0.0) + jnp.log(1.0 + jnp.exp(-jnp.abs(z)))
        if mask is not None:
            sp = jnp.where(mask, sp, 0.0)
        tk = z.shape[1]
        suffixes.append(_dot(sp.astype(BF16), tri[:tk, :tk]))
    out = []
    for z, suffix, (_, vs, mask) in zip(zs, suffixes, blocks):
        p = jnp.exp(z - suffix)
        if mask is not None:
            p = jnp.where(mask, p, 0.0)
        pb = p.astype(BF16)
        pv = jnp.concatenate([_dot(pb[h * tq:(h + 1) * tq], v) for h, v in enumerate(vs)], axis=0)
        out.append((pv, suffix[:, :1]))
    return out


def _sb_accumulate(carry_ref, acc_ref, blocks):
    neg_carry = carry_ref[...]
    acc = acc_ref[...]
    for pv, row_sum in blocks:
        acc = acc + jnp.exp(neg_carry) * pv
        neg_carry = neg_carry - row_sum
    carry_ref[...] = neg_carry
    acc_ref[...] = acc


def _sb_step(qs, ks, vs, carry_ref, acc_ref, tri, mask):
    _sb_accumulate(carry_ref, acc_ref, [_sb_block(qs, ks, vs, tri, mask)])


def _attn_prompt_kernel(q_ref, k_ref, v_ref, mk_ref, mv_ref, tri_ref, _o_in_ref, o_ref, carry_ref, acc_ref,
                        meta_pv_ref, meta_sum_ref, *, t, hg):
    i = pl.program_id(2)
    tri = tri_ref[...]
    heads = range(hg)
    qs = [q_ref[:, _lanes(h)] for h in heads]
    carry_ref[...] = jnp.zeros_like(carry_ref)
    acc_ref[...] = jnp.zeros_like(acc_ref)

    def token_block(c, mask=None):
        off = pl.multiple_of(c * t, t)
        return ([k_ref[pl.ds(off, t), _lanes(h)] for h in heads],
                [v_ref[pl.ds(off, t), _lanes(h)] for h in heads], mask)

    row = lax.broadcasted_iota(jnp.int32, (hg * t, t), 0) & (t - 1)
    col = lax.broadcasted_iota(jnp.int32, (hg * t, t), 1)
    mcol = lax.broadcasted_iota(jnp.int32, (hg * t, mk_ref.shape[0]), 1)
    diag = token_block(i, col < row)
    meta = ([mk_ref[:, _lanes(h)] for h in heads], [mv_ref[:, _lanes(h)] for h in heads], mcol < N_META)

    def head_group(token_blocks):
        *done, (meta_pv, meta_sum) = _sb_blocks(qs, token_blocks + [meta], tri)
        _sb_accumulate(carry_ref, acc_ref, done)
        meta_pv_ref[...] = meta_pv
        meta_sum_ref[...] = meta_sum

    for extra in range(LOOP_BLOCKS):
        @pl.when(i % LOOP_BLOCKS == extra)
        def _():
            head_group([diag] + [token_block(i - 1 - n) for n in range(extra)])

    top = i - i % LOOP_BLOCKS

    def body(s, _):
        first = top - 1 - LOOP_BLOCKS * s
        _sb_accumulate(carry_ref, acc_ref,
                       _sb_blocks(qs, [token_block(first - n) for n in range(LOOP_BLOCKS)], tri))
        return 0

    lax.fori_loop(0, top // LOOP_BLOCKS, body, 0)
    _sb_accumulate(carry_ref, acc_ref, [(meta_pv_ref[...], meta_sum_ref[...])])
    for h in heads:
        o_ref[:, _lanes(h)] = acc_ref[h * t:(h + 1) * t, :].astype(o_ref.dtype)


def _attn_prompt(q, kb, vb, tri, o_all, *, n_batch, seq, n_heads, meta_blk, hg):
    t = tri.shape[0]
    assert t & (t - 1) == 0 and seq % t == 0 and n_heads % hg == 0
    n_q = seq // t
    w = hg * HEAD_DIM
    return pl.pallas_call(
        functools.partial(_attn_prompt_kernel, t=t, hg=hg),
        grid=(n_batch, n_heads // hg, n_q),
        in_specs=[
            pl.BlockSpec((t, w), lambda b, g, i: (b * n_q + i, g)),
            pl.BlockSpec((seq, w), lambda b, g, i: (b, g)),
            pl.BlockSpec((seq, w), lambda b, g, i: (b, g)),
            pl.BlockSpec((ROW_BLOCK, w), lambda b, g, i: (meta_blk, g)),
            pl.BlockSpec((ROW_BLOCK, w), lambda b, g, i: (meta_blk, g)),
            pl.BlockSpec(tri.shape, lambda b, g, i: (0, 0)),
            pl.BlockSpec(memory_space=pl.ANY),
        ],
        out_specs=pl.BlockSpec((t, w), lambda b, g, i: (b * n_q + i, g)),
        out_shape=jax.ShapeDtypeStruct(o_all.shape, o_all.dtype),
        input_output_aliases={6: 0},
        scratch_shapes=[pltpu.VMEM((hg * t, 1), F32), pltpu.VMEM((hg * t, HEAD_DIM), F32),
                        pltpu.VMEM((hg * t, HEAD_DIM), F32), pltpu.VMEM((hg * t, 1), F32)],
        compiler_params=pltpu.CompilerParams(
            dimension_semantics=("arbitrary", "arbitrary", "arbitrary"), vmem_limit_bytes=V7X_VMEM_LIMIT),
        name="attn_prompt",
    )(q, kb, vb, kb, vb, tri, o_all)


def _attn_meta_kernel(q_ref, k_ref, v_ref, tri_ref, _o_in_ref, o_ref, carry_ref, acc_ref, *, n_heads):
    n = q_ref.shape[0]
    carry_ref[...] = jnp.zeros_like(carry_ref)
    acc_ref[...] = jnp.zeros_like(acc_ref)
    row = lax.broadcasted_iota(jnp.int32, (n_heads * n, n), 0) & (n - 1)
    col = lax.broadcasted_iota(jnp.int32, (n_heads * n, n), 1)
    mask = (col < row) & (col < N_META)
    heads = range(n_heads)
    _sb_step([q_ref[:, _lanes(h)] for h in heads], [k_ref[:, _lanes(h)] for h in heads],
             [v_ref[:, _lanes(h)] for h in heads], carry_ref, acc_ref, tri_ref[...], mask)
    for h in heads:
        o_ref[:, _lanes(h)] = acc_ref[h * n:(h + 1) * n, :].astype(o_ref.dtype)


def _attn_meta(q, kb, vb, tri, o_all, *, n_heads, meta_blk):
    w = n_heads * HEAD_DIM
    blk = pl.BlockSpec((ROW_BLOCK, w), lambda i: (meta_blk, 0))
    return pl.pallas_call(
        functools.partial(_attn_meta_kernel, n_heads=n_heads),
        grid=(1,),
        in_specs=[blk, blk, blk, pl.BlockSpec(tri.shape, lambda i: (0, 0)), pl.BlockSpec(memory_space=pl.ANY)],
        out_specs=blk,
        out_shape=jax.ShapeDtypeStruct(o_all.shape, o_all.dtype),
        input_output_aliases={4: 0},
        scratch_shapes=[pltpu.VMEM((n_heads * ROW_BLOCK, 1), F32), pltpu.VMEM((n_heads * ROW_BLOCK, HEAD_DIM), F32)],
        compiler_params=pltpu.CompilerParams(dimension_semantics=("arbitrary",)),
        name="attn_meta",
    )(q, kb, vb, tri, o_all)


def _attn_sample_kernel(q_ref, nk_ref, nv_ref, ck_ref, cv_ref, tri_ref, _o_in_ref, o_ref, carry_ref, acc_ref,
                        *, n_heads, tk, n_chunks):
    t = pl.program_id(1)
    lq = q_ref.shape[0]
    heads = range(n_heads)
    tri = tri_ref[...]
    qs = [q_ref[:, _lanes(h)] for h in heads]

    @pl.when(t == 0)
    def _():
        carry_ref[...] = jnp.zeros_like(carry_ref)
        acc_ref[...] = jnp.zeros_like(acc_ref)
        row = lax.broadcasted_iota(jnp.int32, (n_heads * lq, lq), 0) & (lq - 1)
        col = lax.broadcasted_iota(jnp.int32, (n_heads * lq, lq), 1)
        _sb_step(qs, [nk_ref[:, _lanes(h)] for h in heads], [nv_ref[:, _lanes(h)] for h in heads],
                 carry_ref, acc_ref, tri, col < row)

    n_sub = ck_ref.shape[0] // (tk * n_heads)

    def sub_block(s):
        head_rows = lambda h: pl.ds(s * tk * n_heads + h, tk, stride=n_heads)
        return ([ck_ref[head_rows(h), :].astype(BF16) for h in heads],
                [cv_ref[head_rows(h), :].astype(BF16) for h in heads], None)

    order = list(reversed(range(n_sub)))
    results = []
    for g in range(0, n_sub, LOOP_BLOCKS):
        results += _sb_blocks(qs, [sub_block(s) for s in order[g:g + LOOP_BLOCKS]], tri)
    _sb_accumulate(carry_ref, acc_ref, results)

    @pl.when(t == n_chunks - 1)
    def _():
        for h in heads:
            o_ref[:, _lanes(h)] = acc_ref[h * lq:(h + 1) * lq, :].astype(o_ref.dtype)


def _attn_sample(q, kb, vb, cache_k, cache_v, tri, o_all, *, layer, lq, row0, chunk):
    _, n_batch, past, n_heads, _ = cache_k.shape
    assert lq & (lq - 1) == 0
    tk = tri.shape[0]
    width = n_heads * HEAD_DIM
    n_chunks = past // chunk
    blk0 = row0 // lq
    new_blk = pl.BlockSpec((lq, width), lambda b, t: (blk0 + b, 0))
    cache_rows = lambda c: c.reshape(-1, HEAD_DIM)
    cache_blk = pl.BlockSpec((chunk * n_heads, HEAD_DIM),
                             lambda b, t: ((layer * n_batch + b) * n_chunks + n_chunks - 1 - t, 0))
    return pl.pallas_call(
        functools.partial(_attn_sample_kernel, n_heads=n_heads, tk=tk, n_chunks=n_chunks),
        grid=(n_batch, n_chunks),
        in_specs=[new_blk, new_blk, new_blk, cache_blk, cache_blk, pl.BlockSpec(tri.shape, lambda b, t: (0, 0)),
                  pl.BlockSpec(memory_space=pl.ANY)],
        out_specs=new_blk,
        out_shape=jax.ShapeDtypeStruct(o_all.shape, o_all.dtype),
        input_output_aliases={6: 0},
        scratch_shapes=[pltpu.VMEM((n_heads * lq, 1), F32), pltpu.VMEM((n_heads * lq, HEAD_DIM), F32)],
        compiler_params=pltpu.CompilerParams(
            dimension_semantics=("arbitrary", "arbitrary"), vmem_limit_bytes=V7X_VMEM_LIMIT),
        name="attn_sample",
    )(q, kb, vb, cache_rows(cache_k), cache_rows(cache_v), tri, o_all)


def _mixout_kernel(osb_ref, u_ref, halo_ref, h_ref, wp_ref, ps_ref, wo_ref, g_ref, o_ref, ext_ref, cat_ref,
                   *, meta_seq, sb_width):
    n_seq, lt, pw = u_ref.shape
    cg = pw // len(POOL_WINDOWS)
    rows = n_seq * lt
    ext_ref[:, :HALO, :] = halo_ref[...]
    ext_ref[:, HALO:, :] = u_ref[...]
    cat_ref[:, :sb_width] = osb_ref[...]

    seq_id = pl.program_id(0) * n_seq + lax.broadcasted_iota(jnp.int32, (n_seq, 1, 1), 0)
    n_hist = jnp.where(seq_id >= meta_seq, 0, POOL_HIST)
    n_before = (n_hist + lax.broadcasted_iota(jnp.int32, (n_seq, lt, 1), 1)).astype(F32)

    for g, w in enumerate(POOL_WINDOWS):
        lanes = pl.ds(g * cg, cg)
        cur = ext_ref[:, pl.ds(HALO, lt), lanes]
        s = cur
        for back in range(1, w):
            s = s + ext_ref[:, pl.ds(HALO - back, lt), lanes]
        cnt = jnp.minimum(float(w), n_before + 1.0)
        dlt = (s / cnt - cur).reshape(rows, cg).astype(BF16)
        y = _dot(dlt, wp_ref[g]) * ps_ref[:, lanes]
        cat_ref[:, pl.ds(sb_width + g * cg, cg)] = y.astype(BF16)

    mixed = _dot(cat_ref[...], wo_ref[...])
    o_ref[...] = h_ref[...] + _rms(mixed, g_ref[...])


def _mixout(o_sb, u, halo, h, w_pool, pool_scale, w_out, g, *, layer, tm):
    m, d = h.shape
    pw = u.shape[1]
    sb_width = o_sb.shape[1]
    seq_per = tm // SEQ_TILE
    u3 = u.reshape(m // SEQ_TILE, SEQ_TILE, pw)
    return pl.pallas_call(
        functools.partial(_mixout_kernel, meta_seq=(m - ROW_BLOCK) // SEQ_TILE, sb_width=sb_width),
        grid=(m // tm,),
        in_specs=[
            pl.BlockSpec((tm, sb_width), lambda i: (i, 0)),
            pl.BlockSpec((seq_per, SEQ_TILE, pw), lambda i: (i, 0, 0)),
            pl.BlockSpec((seq_per, HALO, pw), lambda i: (i, 0, 0)),
            pl.BlockSpec((tm, d), lambda i: (i, 0)),
            pl.BlockSpec((None,) + w_pool.shape[1:], lambda i: (layer, 0, 0, 0)),
            pl.BlockSpec((1, pw), lambda i: (0, 0)),
            pl.BlockSpec((None,) + w_out.shape[1:], lambda i: (layer, 0, 0)),
            pl.BlockSpec((1, d), lambda i: (0, 0)),
        ],
        out_specs=pl.BlockSpec((tm, d), lambda i: (i, 0)),
        out_shape=jax.ShapeDtypeStruct((m, d), F32),
        scratch_shapes=[pltpu.VMEM((seq_per, HALO + SEQ_TILE, pw), F32),
                        pltpu.VMEM((tm, sb_width + pw), BF16)],
        compiler_params=pltpu.CompilerParams(
            dimension_semantics=("arbitrary",), vmem_limit_bytes=V7X_VMEM_LIMIT),
        name="mixout",
    )(o_sb, u3, halo, h, w_pool, pool_scale, w_out, g)


def _row_tile(m, cap):
    blocks = m // ROW_BLOCK
    best = 1
    for dvs in range(1, blocks + 1):
        if blocks % dvs == 0 and dvs * ROW_BLOCK <= cap:
            best = dvs
    return best * ROW_BLOCK


def _col_tile(n, cap):
    best = HEAD_DIM
    for t in range(HEAD_DIM, cap + 1, HEAD_DIM):
        if n % t == 0:
            best = t
    return best


def kernel(x_prompt, x_sample, cache_k, cache_v, state_pool, meta_tokens, w_in, w_out, w_pool, pool_scale,
           norm_gains, ffn1_gate, ffn1_up, ffn1_down, ffn2_gate, ffn2_up, ffn2_down):
    n_b, seq, d = x_prompt.shape
    s_b, s_len, _ = x_sample.shape
    depth, _, past, n_heads, head_dim = cache_k.shape
    pw = state_pool.shape[-1]
    sbw = n_heads * head_dim
    assert head_dim == HEAD_DIM and meta_tokens.shape[0] == N_META and sbw == pw
    assert w_in.shape[-1] == 3 * sbw + pw and state_pool.shape[2] == POOL_HIST
    assert seq % SEQ_TILE == 0 and s_len == SEQ_TILE and seq >= POOL_HIST and N_META == HALO

    r_p = n_b * seq
    r_t = r_p + s_b * s_len
    assert r_t % ROW_BLOCK == 0
    m = r_t + ROW_BLOCK
    meta_blk = r_t // ROW_BLOCK

    tk = 256
    assert seq % tk == 0 and past % tk == 0
    chunk = 1024 if past % 1024 == 0 else tk
    hg = 4 if n_heads % 4 == 0 else 1
    tri = (lax.broadcasted_iota(jnp.int32, (tk, tk), 0) >= lax.broadcasted_iota(jnp.int32, (tk, tk), 1)).astype(BF16)

    x = jnp.concatenate([
        x_prompt.reshape(r_p, d), x_sample.reshape(s_b * s_len, d), meta_tokens.astype(F32),
        jnp.zeros((ROW_BLOCK - N_META, d), F32)], axis=0)

    tm_ffn = _row_tile(m, 1152)
    tf = _col_tile(ffn1_gate.shape[-1], 512)
    tm_proj = _row_tile(m, 576)
    n_tiles_p = seq // SEQ_TILE

    w_in_bf, w_out_bf, w_pool_bf = (w.astype(BF16) for w in (w_in, w_out, w_pool))
    seq_of = lambda row: row // SEQ_TILE
    outs = [[] for _ in range(6)]
    for l in range(depth):
        gains = norm_gains[l].astype(F32)
        gain = lambda n: gains[n][None, :]

        x = _ffn(x, gain(0), gain(1), ffn1_gate, ffn1_up, ffn1_down, layer=l, tm=tm_ffn, tf=tf)
        q, kb, vb, k, v, u = _inproj(x, gain(2), w_in_bf, layer=l, tm=tm_proj, width=sbw)

        o_sb = jnp.zeros((m, sbw), BF16)
        o_sb = _attn_prompt(q, kb, vb, tri, o_sb, n_batch=n_b, seq=seq, n_heads=n_heads, meta_blk=meta_blk, hg=hg)
        o_sb = _attn_sample(q, kb, vb, cache_k, cache_v, tri, o_sb, layer=l, lq=s_len, row0=r_p, chunk=chunk)
        o_sb = _attn_meta(q, kb, vb, tri, o_sb, n_heads=n_heads, meta_blk=meta_blk)

        tails = u.reshape(seq_of(m), SEQ_TILE, pw)[:, SEQ_TILE - HALO:]
        halo_p = jnp.concatenate(
            [part for b in range(n_b)
             for part in (u[r_t:r_t + N_META][None], tails[b * n_tiles_p:(b + 1) * n_tiles_p - 1])], axis=0)
        halo_s = jnp.pad(state_pool[l].astype(F32), ((0, 0), (HALO - POOL_HIST, 0), (0, 0)))
        halo = jnp.concatenate([halo_p, halo_s, jnp.zeros((ROW_BLOCK // SEQ_TILE, HALO, pw), F32)], axis=0)

        x = _mixout(o_sb, u, halo, x, w_pool_bf, pool_scale[l].astype(F32)[None, :], w_out_bf, gain(3),
                    layer=l, tm=tm_proj)
        x = _ffn(x, gain(4), gain(5), ffn2_gate, ffn2_up, ffn2_down, layer=l, tm=tm_ffn, tf=tf)

        k = k.reshape(m, n_heads, HEAD_DIM)
        v = v.reshape(m, n_heads, HEAD_DIM)
        for a, pieces in ((k, outs[0]), (v, outs[1])):
            for b in range(n_b):
                pieces += [a[r_t:r_t + N_META], a[b * seq:(b + 1) * seq]]
        outs[2].append(jnp.stack([u[(b + 1) * seq - POOL_HIST:(b + 1) * seq] for b in range(n_b)]))
        outs[3].append(k[r_p:r_t].reshape(s_b, s_len, n_heads, HEAD_DIM))
        outs[4].append(v[r_p:r_t].reshape(s_b, s_len, n_heads, HEAD_DIM))
        outs[5].append(tails[seq_of(r_p):seq_of(r_t), HALO - POOL_HIST:])

    y_prompt = x[:r_p].reshape(n_b, seq, d)
    y_sample = x[r_p:r_t].reshape(s_b, s_len, d)
    prompt_cache = lambda pieces: jnp.concatenate(pieces, axis=0).reshape(
        depth, n_b, N_META + seq, n_heads, HEAD_DIM)
    return (y_prompt, y_sample, prompt_cache(outs[0]), prompt_cache(outs[1])) + tuple(
        jnp.stack(o) for o in outs[2:])
```

```python
import functools

import jax
import jax.numpy as jnp
from jax import lax
from jax.experimental import pallas as pl
from jax.experimental.pallas import tpu as pltpu

F32 = jnp.float32
BF16 = jnp.bfloat16

RMS_EPS = 1e-6
LOG2_E = 1.4426950408889634
HEAD_DIM = 128
MXU_WIDTH = 256
N_META = 16
POOL_WINDOWS = (2, 4, 8, 16)
POOL_HIST = max(POOL_WINDOWS) - 1
HALO = 16
SEQ_TILE = 64
ROW_BLOCK = 128
LOOP_BLOCKS = 2
V7X_VMEM_BYTES = 64 * 1024 * 1024
V7X_VMEM_LIMIT = V7X_VMEM_BYTES - 4 * 1024 * 1024


def _rms(x, g):
    ms = jnp.mean(x * x, axis=-1, keepdims=True)
    return x * lax.rsqrt(ms + RMS_EPS) * g


def _dot(a, b):
    return jnp.dot(a, b, preferred_element_type=F32)


def _ffn_kernel(x_ref, gi_ref, go_ref, wg_ref, wu_ref, wd_ref, o_ref, xn_ref, *, n_f):
    j = pl.program_id(1)

    @pl.when(j == 0)
    def _():
        xn_ref[...] = _rms(x_ref[...], gi_ref[...]).astype(BF16)
        o_ref[...] = jnp.zeros_like(o_ref)

    xn = xn_ref[...]
    tf = wg_ref.shape[1]
    hidden = []
    for c in range(0, tf, MXU_WIDTH):
        g = _dot(xn, wg_ref[:, c:c + MXU_WIDTH].astype(BF16))
        u = _dot(xn, wu_ref[:, c:c + MXU_WIDTH].astype(BF16))
        hidden.append((g * jax.nn.sigmoid(g) * u).astype(BF16))
    hidden = jnp.concatenate(hidden, axis=1)
    for c in range(0, o_ref.shape[1], MXU_WIDTH):
        o_ref[:, c:c + MXU_WIDTH] += _dot(hidden, wd_ref[:, c:c + MXU_WIDTH].astype(BF16))

    @pl.when(j == n_f - 1)
    def _():
        go = go_ref[...]

        def finish_rows(r, _):
            rows = pl.ds(pl.multiple_of(r * ROW_BLOCK, ROW_BLOCK), ROW_BLOCK)
            o_ref[rows, :] = x_ref[rows, :] + 0.5 * _rms(o_ref[rows, :], go)
            return 0

        lax.fori_loop(0, o_ref.shape[0] // ROW_BLOCK, finish_rows, 0)


def _ffn(x, gi, go, wg, wu, wd, *, layer, tm, tf):
    m, d = x.shape
    f = wg.shape[2]
    n_f = f // tf
    return pl.pallas_call(
        functools.partial(_ffn_kernel, n_f=n_f),
        grid=(m // tm, n_f),
        in_specs=[
            pl.BlockSpec((tm, d), lambda i, j: (i, 0), pipeline_mode=pl.Buffered(1)),
            pl.BlockSpec((1, d), lambda i, j: (0, 0)),
            pl.BlockSpec((1, d), lambda i, j: (0, 0)),
            pl.BlockSpec((None, d, tf), lambda i, j: (layer, 0, j)),
            pl.BlockSpec((None, d, tf), lambda i, j: (layer, 0, j)),
            pl.BlockSpec((None, tf, d), lambda i, j: (layer, j, 0)),
        ],
        out_specs=pl.BlockSpec((tm, d), lambda i, j: (i, 0), pipeline_mode=pl.Buffered(1)),
        out_shape=jax.ShapeDtypeStruct((m, d), F32),
        scratch_shapes=[pltpu.VMEM((tm, d), BF16)],
        compiler_params=pltpu.CompilerParams(
            dimension_semantics=("arbitrary", "arbitrary"), vmem_limit_bytes=V7X_VMEM_LIMIT),
        name="ffn",
    )(x, gi, go, wg, wu, wd)


def _inproj_kernel(x_ref, g_ref, w_ref, q_ref, kb_ref, vb_ref, k_ref, v_ref, u_ref, *, q_scale, n_heads):
    tm = x_ref.shape[0]
    width = q_ref.shape[1]
    xn = _rms(x_ref[...], g_ref[...]).astype(BF16)

    def proj(c):
        return _dot(xn, w_ref[:, c * width:(c + 1) * width])

    q_ref[...] = (proj(0) * q_scale).astype(BF16)
    for c, ref, bf_ref in ((1, k_ref, kb_ref), (2, v_ref, vb_ref)):
        p = proj(c)
        for h in range(n_heads):
            ref[pl.ds(h, tm, stride=n_heads), :] = p[:, _lanes(h)]
        bf_ref[...] = p.astype(BF16)
    u_ref[...] = proj(3)


def _inproj(x, g, w_in, *, layer, tm, width):
    m, d = x.shape
    n_heads = width // HEAD_DIM
    row_blk = lambda i: (i, 0)
    wide = pl.BlockSpec((tm, width), row_blk)
    by_head = pl.BlockSpec((tm * n_heads, HEAD_DIM), row_blk)
    out_bf = jax.ShapeDtypeStruct((m, width), BF16)
    out_heads = jax.ShapeDtypeStruct((m * n_heads, HEAD_DIM), F32)
    return pl.pallas_call(
        functools.partial(_inproj_kernel, q_scale=HEAD_DIM ** -0.5, n_heads=n_heads),
        grid=(m // tm,),
        in_specs=[
            pl.BlockSpec((tm, d), row_blk),
            pl.BlockSpec((1, d), lambda i: (0, 0)),
            pl.BlockSpec((None,) + w_in.shape[1:], lambda i: (layer, 0, 0), pipeline_mode=pl.Buffered(1)),
        ],
        out_specs=[wide, wide, wide, by_head, by_head, wide],
        out_shape=[out_bf, out_bf, out_bf, out_heads, out_heads, jax.ShapeDtypeStruct((m, width), F32)],
        compiler_params=pltpu.CompilerParams(
            dimension_semantics=("arbitrary",), vmem_limit_bytes=V7X_VMEM_LIMIT),
        name="inproj",
    )(x, g, w_in)


def _lanes(h):
    return slice(h * HEAD_DIM, (h + 1) * HEAD_DIM)


def _sb_block(qs, ks, vs, tri, mask):
    return _sb_blocks(qs, [(ks, vs, mask)], tri)[0]


def _sb_blocks(qs, blocks, tri):
    tq = qs[0].shape[0]
    zs = [jnp.concatenate(
        [lax.dot_general(q, k, (((1,), (1,)), ((), ())), preferred_element_type=F32) for q, k in zip(qs, ks)],
        axis=0) for ks, _, _ in blocks]
    suffixes = []
    for z, (_, _, mask) in zip(zs, blocks):
        sp = jnp.maximum(z, 0.0) + jnp.log(1.0 + jnp.exp2(jnp.abs(z) * -LOG2_E))
        if mask is not None:
            sp = jnp.where(mask, sp, 0.0)
        tk = z.shape[1]
        suffixes.append(_dot(sp.astype(BF16), tri[:tk, :tk]))
    out = []
    for z, suffix, (_, vs, mask) in zip(zs, suffixes, blocks):
        arg = z - suffix
        if mask is not None:
            arg = jnp.where(mask, arg, -jnp.inf)
        pb = jnp.exp(arg.astype(BF16))
        pv = jnp.concatenate([_dot(pb[h * tq:(h + 1) * tq], v) for h, v in enumerate(vs)], axis=0)
        out.append((pv, suffix[:, :1]))
    return out


def _sb_accumulate(carry_ref, acc_ref, blocks):
    neg_carry = carry_ref[...]
    acc = acc_ref[...]
    for pv, row_sum in blocks:
        acc = acc + jnp.exp(neg_carry) * pv
        neg_carry = neg_carry - row_sum
    carry_ref[...] = neg_carry
    acc_ref[...] = acc


def _sb_step(qs, ks, vs, carry_ref, acc_ref, tri, mask):
    _sb_accumulate(carry_ref, acc_ref, [_sb_block(qs, ks, vs, tri, mask)])


def _attn_prompt_kernel(q_ref, k_ref, v_ref, mk_ref, mv_ref, tri_ref, _o_in_ref, o_ref, carry_ref, acc_ref,
                        meta_pv_ref, meta_sum_ref, *, t, hg):
    i = pl.program_id(2)
    tri = tri_ref[...]
    heads = range(hg)
    qs = [q_ref[:, _lanes(h)] for h in heads]
    carry_ref[...] = jnp.zeros_like(carry_ref)
    acc_ref[...] = jnp.zeros_like(acc_ref)

    def token_block(c, mask=None):
        off = pl.multiple_of(c * t, t)
        return ([k_ref[pl.ds(off, t), _lanes(h)] for h in heads],
                [v_ref[pl.ds(off, t), _lanes(h)] for h in heads], mask)

    row = lax.broadcasted_iota(jnp.int32, (hg * t, t), 0) & (t - 1)
    col = lax.broadcasted_iota(jnp.int32, (hg * t, t), 1)
    mcol = lax.broadcasted_iota(jnp.int32, (hg * t, mk_ref.shape[0]), 1)
    diag = token_block(i, col < row)
    meta = ([mk_ref[:, _lanes(h)] for h in heads], [mv_ref[:, _lanes(h)] for h in heads], mcol < N_META)

    def head_group(token_blocks):
        *done, (meta_pv, meta_sum) = _sb_blocks(qs, token_blocks + [meta], tri)
        _sb_accumulate(carry_ref, acc_ref, done)
        meta_pv_ref[...] = meta_pv
        meta_sum_ref[...] = meta_sum

    for extra in range(LOOP_BLOCKS):
        @pl.when(i % LOOP_BLOCKS == extra)
        def _():
            head_group([diag] + [token_block(i - 1 - n) for n in range(extra)])

    top = i - i % LOOP_BLOCKS

    def body(s, _):
        first = top - 1 - LOOP_BLOCKS * s
        _sb_accumulate(carry_ref, acc_ref,
                       _sb_blocks(qs, [token_block(first - n) for n in range(LOOP_BLOCKS)], tri))
        return 0

    lax.fori_loop(0, top // LOOP_BLOCKS, body, 0)
    _sb_accumulate(carry_ref, acc_ref, [(meta_pv_ref[...], meta_sum_ref[...])])
    for h in heads:
        o_ref[:, _lanes(h)] = acc_ref[h * t:(h + 1) * t, :].astype(o_ref.dtype)


def _attn_prompt(q, kb, vb, tri, o_all, *, n_batch, seq, n_heads, meta_blk, hg):
    t = tri.shape[0]
    assert t & (t - 1) == 0 and seq % t == 0 and n_heads % hg == 0
    n_q = seq // t
    w = hg * HEAD_DIM
    return pl.pallas_call(
        functools.partial(_attn_prompt_kernel, t=t, hg=hg),
        grid=(n_batch, n_heads // hg, n_q),
        in_specs=[
            pl.BlockSpec((t, w), lambda b, g, i: (b * n_q + i, g)),
            pl.BlockSpec((seq, w), lambda b, g, i: (b, g)),
            pl.BlockSpec((seq, w), lambda b, g, i: (b, g)),
            pl.BlockSpec((ROW_BLOCK, w), lambda b, g, i: (meta_blk, g)),
            pl.BlockSpec((ROW_BLOCK, w), lambda b, g, i: (meta_blk, g)),
            pl.BlockSpec(tri.shape, lambda b, g, i: (0, 0)),
            pl.BlockSpec(memory_space=pl.ANY),
        ],
        out_specs=pl.BlockSpec((t, w), lambda b, g, i: (b * n_q + i, g)),
        out_shape=jax.ShapeDtypeStruct(o_all.shape, o_all.dtype),
        input_output_aliases={6: 0},
        scratch_shapes=[pltpu.VMEM((hg * t, 1), F32), pltpu.VMEM((hg * t, HEAD_DIM), F32),
                        pltpu.VMEM((hg * t, HEAD_DIM), F32), pltpu.VMEM((hg * t, 1), F32)],
        compiler_params=pltpu.CompilerParams(
            dimension_semantics=("arbitrary", "arbitrary", "arbitrary"), vmem_limit_bytes=V7X_VMEM_LIMIT),
        name="attn_prompt",
    )(q, kb, vb, kb, vb, tri, o_all)


def _attn_meta_kernel(q_ref, k_ref, v_ref, tri_ref, _o_in_ref, o_ref, carry_ref, acc_ref, *, n_heads):
    n = q_ref.shape[0]
    carry_ref[...] = jnp.zeros_like(carry_ref)
    acc_ref[...] = jnp.zeros_like(acc_ref)
    row = lax.broadcasted_iota(jnp.int32, (n_heads * n, n), 0) & (n - 1)
    col = lax.broadcasted_iota(jnp.int32, (n_heads * n, n), 1)
    mask = (col < row) & (col < N_META)
    heads = range(n_heads)
    _sb_step([q_ref[:, _lanes(h)] for h in heads], [k_ref[:, _lanes(h)] for h in heads],
             [v_ref[:, _lanes(h)] for h in heads], carry_ref, acc_ref, tri_ref[...], mask)
    for h in heads:
        o_ref[:, _lanes(h)] = acc_ref[h * n:(h + 1) * n, :].astype(o_ref.dtype)


def _attn_meta(q, kb, vb, tri, o_all, *, n_heads, meta_blk):
    w = n_heads * HEAD_DIM
    blk = pl.BlockSpec((ROW_BLOCK, w), lambda i: (meta_blk, 0))
    return pl.pallas_call(
        functools.partial(_attn_meta_kernel, n_heads=n_heads),
        grid=(1,),
        in_specs=[blk, blk, blk, pl.BlockSpec(tri.shape, lambda i: (0, 0)), pl.BlockSpec(memory_space=pl.ANY)],
        out_specs=blk,
        out_shape=jax.ShapeDtypeStruct(o_all.shape, o_all.dtype),
        input_output_aliases={4: 0},
        scratch_shapes=[pltpu.VMEM((n_heads * ROW_BLOCK, 1), F32), pltpu.VMEM((n_heads * ROW_BLOCK, HEAD_DIM), F32)],
        compiler_params=pltpu.CompilerParams(dimension_semantics=("arbitrary",)),
        name="attn_meta",
    )(q, kb, vb, tri, o_all)


def _attn_sample_kernel(q_ref, nk_ref, nv_ref, ck_ref, cv_ref, tri_ref, _o_in_ref, o_ref, carry_ref, acc_ref,
                        *, n_heads, tk, n_chunks):
    t = pl.program_id(1)
    lq = q_ref.shape[0]
    heads = range(n_heads)
    tri = tri_ref[...]
    qs = [q_ref[:, _lanes(h)] for h in heads]

    @pl.when(t == 0)
    def _():
        carry_ref[...] = jnp.zeros_like(carry_ref)
        acc_ref[...] = jnp.zeros_like(acc_ref)
        row = lax.broadcasted_iota(jnp.int32, (n_heads * lq, lq), 0) & (lq - 1)
        col = lax.broadcasted_iota(jnp.int32, (n_heads * lq, lq), 1)
        _sb_step(qs, [nk_ref[:, _lanes(h)] for h in heads], [nv_ref[:, _lanes(h)] for h in heads],
                 carry_ref, acc_ref, tri, col < row)

    n_sub = ck_ref.shape[0] // (tk * n_heads)

    def sub_block(s):
        head_rows = lambda h: pl.ds(s * tk * n_heads + h, tk, stride=n_heads)
        return ([ck_ref[head_rows(h), :].astype(BF16) for h in heads],
                [cv_ref[head_rows(h), :].astype(BF16) for h in heads], None)

    order = list(reversed(range(n_sub)))
    results = []
    for g in range(0, n_sub, LOOP_BLOCKS):
        results += _sb_blocks(qs, [sub_block(s) for s in order[g:g + LOOP_BLOCKS]], tri)
    _sb_accumulate(carry_ref, acc_ref, results)

    @pl.when(t == n_chunks - 1)
    def _():
        for h in heads:
            o_ref[:, _lanes(h)] = acc_ref[h * lq:(h + 1) * lq, :].astype(o_ref.dtype)


def _attn_sample(q, kb, vb, cache_k, cache_v, tri, o_all, *, layer, lq, row0, chunk):
    _, n_batch, past, n_heads, _ = cache_k.shape
    assert lq & (lq - 1) == 0
    tk = tri.shape[0]
    width = n_heads * HEAD_DIM
    n_chunks = past // chunk
    blk0 = row0 // lq
    new_blk = pl.BlockSpec((lq, width), lambda b, t: (blk0 + b, 0))
    cache_rows = lambda c: c.reshape(-1, HEAD_DIM)
    cache_blk = pl.BlockSpec((chunk * n_heads, HEAD_DIM),
                             lambda b, t: ((layer * n_batch + b) * n_chunks + n_chunks - 1 - t, 0))
    return pl.pallas_call(
        functools.partial(_attn_sample_kernel, n_heads=n_heads, tk=tk, n_chunks=n_chunks),
        grid=(n_batch, n_chunks),
        in_specs=[new_blk, new_blk, new_blk, cache_blk, cache_blk, pl.BlockSpec(tri.shape, lambda b, t: (0, 0)),
                  pl.BlockSpec(memory_space=pl.ANY)],
        out_specs=new_blk,
        out_shape=jax.ShapeDtypeStruct(o_all.shape, o_all.dtype),
        input_output_aliases={6: 0},
        scratch_shapes=[pltpu.VMEM((n_heads * lq, 1), F32), pltpu.VMEM((n_heads * lq, HEAD_DIM), F32)],
        compiler_params=pltpu.CompilerParams(
            dimension_semantics=("arbitrary", "arbitrary"), vmem_limit_bytes=V7X_VMEM_LIMIT),
        name="attn_sample",
    )(q, kb, vb, cache_rows(cache_k), cache_rows(cache_v), tri, o_all)


def _mixout_kernel(osb_ref, u_ref, halo_ref, h_ref, wp_ref, ps_ref, wo_ref, g_ref, o_ref, ext_ref, cat_ref,
                   *, meta_seq, sb_width):
    n_seq, lt, pw = u_ref.shape
    cg = pw // len(POOL_WINDOWS)
    rows = n_seq * lt
    ext_ref[:, :HALO, :] = halo_ref[...]
    ext_ref[:, HALO:, :] = u_ref[...]
    cat_ref[:, :sb_width] = osb_ref[...]

    seq_id = pl.program_id(0) * n_seq + lax.broadcasted_iota(jnp.int32, (n_seq, 1, 1), 0)
    n_hist = jnp.where(seq_id >= meta_seq, 0, POOL_HIST)
    n_before = (n_hist + lax.broadcasted_iota(jnp.int32, (n_seq, lt, 1), 1)).astype(F32)

    for g, w in enumerate(POOL_WINDOWS):
        lanes = pl.ds(g * cg, cg)
        cur = ext_ref[:, pl.ds(HALO, lt), lanes]
        s = cur
        for back in range(1, w):
            s = s + ext_ref[:, pl.ds(HALO - back, lt), lanes]
        cnt = jnp.minimum(float(w), n_before + 1.0)
        dlt = (s / cnt - cur).reshape(rows, cg).astype(BF16)
        y = _dot(dlt, wp_ref[g]) * ps_ref[:, lanes]
        cat_ref[:, pl.ds(sb_width + g * cg, cg)] = y.astype(BF16)

    mixed = _dot(cat_ref[...], wo_ref[...])
    o_ref[...] = h_ref[...] + _rms(mixed, g_ref[...])


def _mixout(o_sb, u, halo, h, w_pool, pool_scale, w_out, g, *, layer, tm):
    m, d = h.shape
    pw = u.shape[1]
    sb_width = o_sb.shape[1]
    seq_per = tm // SEQ_TILE
    u3 = u.reshape(m // SEQ_TILE, SEQ_TILE, pw)
    return pl.pallas_call(
        functools.partial(_mixout_kernel, meta_seq=(m - ROW_BLOCK) // SEQ_TILE, sb_width=sb_width),
        grid=(m // tm,),
        in_specs=[
            pl.BlockSpec((tm, sb_width), lambda i: (i, 0)),
            pl.BlockSpec((seq_per, SEQ_TILE, pw), lambda i: (i, 0, 0)),
            pl.BlockSpec((seq_per, HALO, pw), lambda i: (i, 0, 0)),
            pl.BlockSpec((tm, d), lambda i: (i, 0)),
            pl.BlockSpec((None,) + w_pool.shape[1:], lambda i: (layer, 0, 0, 0)),
            pl.BlockSpec((1, pw), lambda i: (0, 0)),
            pl.BlockSpec((None,) + w_out.shape[1:], lambda i: (layer, 0, 0)),
            pl.BlockSpec((1, d), lambda i: (0, 0)),
        ],
        out_specs=pl.BlockSpec((tm, d), lambda i: (i, 0)),
        out_shape=jax.ShapeDtypeStruct((m, d), F32),
        scratch_shapes=[pltpu.VMEM((seq_per, HALO + SEQ_TILE, pw), F32),
                        pltpu.VMEM((tm, sb_width + pw), BF16)],
        compiler_params=pltpu.CompilerParams(
            dimension_semantics=("arbitrary",), vmem_limit_bytes=V7X_VMEM_LIMIT),
        name="mixout",
    )(o_sb, u3, halo, h, w_pool, pool_scale, w_out, g)


def _row_tile(m, cap):
    blocks = m // ROW_BLOCK
    best = 1
    for dvs in range(1, blocks + 1):
        if blocks % dvs == 0 and dvs * ROW_BLOCK <= cap:
            best = dvs
    return best * ROW_BLOCK


def _col_tile(n, cap):
    best = HEAD_DIM
    for t in range(HEAD_DIM, cap + 1, HEAD_DIM):
        if n % t == 0:
            best = t
    return best


def kernel(x_prompt, x_sample, cache_k, cache_v, state_pool, meta_tokens, w_in, w_out, w_pool, pool_scale,
           norm_gains, ffn1_gate, ffn1_up, ffn1_down, ffn2_gate, ffn2_up, ffn2_down):
    n_b, seq, d = x_prompt.shape
    s_b, s_len, _ = x_sample.shape
    depth, _, past, n_heads, head_dim = cache_k.shape
    pw = state_pool.shape[-1]
    sbw = n_heads * head_dim
    assert head_dim == HEAD_DIM and meta_tokens.shape[0] == N_META and sbw == pw
    assert w_in.shape[-1] == 3 * sbw + pw and state_pool.shape[2] == POOL_HIST
    assert seq % SEQ_TILE == 0 and s_len == SEQ_TILE and seq >= POOL_HIST and N_META == HALO

    r_p = n_b * seq
    r_t = r_p + s_b * s_len
    assert r_t % ROW_BLOCK == 0
    m = r_t + ROW_BLOCK
    meta_blk = r_t // ROW_BLOCK

    tk = 256
    assert seq % tk == 0 and past % tk == 0
    chunk = 1024 if past % 1024 == 0 else tk
    hg = 4 if n_heads % 4 == 0 else 1
    tri = (lax.broadcasted_iota(jnp.int32, (tk, tk), 0) >= lax.broadcasted_iota(jnp.int32, (tk, tk), 1)).astype(BF16)

    x = jnp.concatenate([
        x_prompt.reshape(r_p, d), x_sample.reshape(s_b * s_len, d), meta_tokens.astype(F32),
        jnp.zeros((ROW_BLOCK - N_META, d), F32)], axis=0)

    tm_ffn = _row_tile(m, 1152)
    tf = _col_tile(ffn1_gate.shape[-1], 512)
    tm_proj = _row_tile(m, 576)
    n_tiles_p = seq // SEQ_TILE

    w_in_bf, w_out_bf, w_pool_bf = (w.astype(BF16) for w in (w_in, w_out, w_pool))
    seq_of = lambda row: row // SEQ_TILE
    outs = [[] for _ in range(6)]
    for l in range(depth):
        gains = norm_gains[l].astype(F32)
        gain = lambda n: gains[n][None, :]

        x = _ffn(x, gain(0), gain(1), ffn1_gate, ffn1_up, ffn1_down, layer=l, tm=tm_ffn, tf=tf)
        q, kb, vb, k, v, u = _inproj(x, gain(2), w_in_bf, layer=l, tm=tm_proj, width=sbw)

        o_sb = jnp.zeros((m, sbw), BF16)
        o_sb = _attn_prompt(q, kb, vb, tri, o_sb, n_batch=n_b, seq=seq, n_heads=n_heads, meta_blk=meta_blk, hg=hg)
        o_sb = _attn_sample(q, kb, vb, cache_k, cache_v, tri, o_sb, layer=l, lq=s_len, row0=r_p, chunk=chunk)
        o_sb = _attn_meta(q, kb, vb, tri, o_sb, n_heads=n_heads, meta_blk=meta_blk)

        tails = u.reshape(seq_of(m), SEQ_TILE, pw)[:, SEQ_TILE - HALO:]
        halo_p = jnp.concatenate(
            [part for b in range(n_b)
             for part in (u[r_t:r_t + N_META][None], tails[b * n_tiles_p:(b + 1) * n_tiles_p - 1])], axis=0)
        halo_s = jnp.pad(state_pool[l].astype(F32), ((0, 0), (HALO - POOL_HIST, 0), (0, 0)))
        halo = jnp.concatenate([halo_p, halo_s, jnp.zeros((ROW_BLOCK // SEQ_TILE, HALO, pw), F32)], axis=0)

        x = _mixout(o_sb, u, halo, x, w_pool_bf, pool_scale[l].astype(F32)[None, :], w_out_bf, gain(3),
                    layer=l, tm=tm_proj)
        x = _ffn(x, gain(4), gain(5), ffn2_gate, ffn2_up, ffn2_down, layer=l, tm=tm_ffn, tf=tf)

        k = k.reshape(m, n_heads, HEAD_DIM)
        v = v.reshape(m, n_heads, HEAD_DIM)
        for a, pieces in ((k, outs[0]), (v, outs[1])):
            for b in range(n_b):
                pieces += [a[r_t:r_t + N_META], a[b * seq:(b + 1) * seq]]
        outs[2].append(jnp.stack([u[(b + 1) * seq - POOL_HIST:(b + 1) * seq] for b in range(n_b)]))
        outs[3].append(k[r_p:r_t].reshape(s_b, s_len, n_heads, HEAD_DIM))
        outs[4].append(v[r_p:r_t].reshape(s_b, s_len, n_heads, HEAD_DIM))
        outs[5].append(tails[seq_of(r_p):seq_of(r_t), HALO - POOL_HIST:])

    y_prompt = x[:r_p].reshape(n_b, seq, d)
    y_sample = x[r_p:r_t].reshape(s_b, s_len, d)
    prompt_cache = lambda pieces: jnp.concatenate(pieces, axis=0).reshape(
        depth, n_b, N_META + seq, n_heads, HEAD_DIM)
    return (y_prompt, y_sample, prompt_cache(outs[0]), prompt_cache(outs[1])) + tuple(
        jnp.stack(o) for o in outs[2:])
```

```python
import functools

import jax
import jax.numpy as jnp
from jax import lax
from jax.experimental import pallas as pl
from jax.experimental.pallas import tpu as pltpu

F32 = jnp.float32
BF16 = jnp.bfloat16

RMS_EPS = 1e-6
LOG2_E = 1.4426950408889634
HEAD_DIM = 128
MXU_WIDTH = 256
N_META = 16
POOL_WINDOWS = (2, 4, 8, 16)
POOL_HIST = max(POOL_WINDOWS) - 1
HALO = 16
SEQ_TILE = 64
ROW_BLOCK = 128
LOOP_BLOCKS = 2
V7X_VMEM_BYTES = 64 * 1024 * 1024
V7X_VMEM_LIMIT = V7X_VMEM_BYTES - 4 * 1024 * 1024


def _rms(x, g):
    ms = jnp.mean(x * x, axis=-1, keepdims=True)
    return x * lax.rsqrt(ms + RMS_EPS) * g


def _dot(a, b):
    return jnp.dot(a, b, preferred_element_type=F32)


def _ffn_kernel(x_ref, gi_ref, go_ref, wg_ref, wu_ref, wd_ref, o_ref, xn_ref, *, n_f):
    j = pl.program_id(1)

    @pl.when(j == 0)
    def _():
        xn_ref[...] = _rms(x_ref[...], gi_ref[...]).astype(BF16)
        o_ref[...] = jnp.zeros_like(o_ref)

    xn = xn_ref[...]
    tf = wg_ref.shape[1]
    hidden = []
    for c in range(0, tf, MXU_WIDTH):
        g = _dot(xn, wg_ref[:, c:c + MXU_WIDTH].astype(BF16))
        u = _dot(xn, wu_ref[:, c:c + MXU_WIDTH].astype(BF16))
        hidden.append((g * jax.nn.sigmoid(g) * u).astype(BF16))
    hidden = jnp.concatenate(hidden, axis=1)
    for c in range(0, o_ref.shape[1], MXU_WIDTH):
        o_ref[:, c:c + MXU_WIDTH] += _dot(hidden, wd_ref[:, c:c + MXU_WIDTH].astype(BF16))

    @pl.when(j == n_f - 1)
    def _():
        go = go_ref[...]

        def finish_rows(r, _):
            rows = pl.ds(pl.multiple_of(r * ROW_BLOCK, ROW_BLOCK), ROW_BLOCK)
            o_ref[rows, :] = x_ref[rows, :] + 0.5 * _rms(o_ref[rows, :], go)
            return 0

        lax.fori_loop(0, o_ref.shape[0] // ROW_BLOCK, finish_rows, 0)


def _ffn(x, gi, go, wg, wu, wd, *, layer, tm, tf):
    m, d = x.shape
    f = wg.shape[2]
    n_f = f // tf
    return pl.pallas_call(
        functools.partial(_ffn_kernel, n_f=n_f),
        grid=(m // tm, n_f),
        in_specs=[
            pl.BlockSpec((tm, d), lambda i, j: (i, 0), pipeline_mode=pl.Buffered(1)),
            pl.BlockSpec((1, d), lambda i, j: (0, 0)),
            pl.BlockSpec((1, d), lambda i, j: (0, 0)),
            pl.BlockSpec((None, d, tf), lambda i, j: (layer, 0, j)),
            pl.BlockSpec((None, d, tf), lambda i, j: (layer, 0, j)),
            pl.BlockSpec((None, tf, d), lambda i, j: (layer, j, 0)),
        ],
        out_specs=pl.BlockSpec((tm, d), lambda i, j: (i, 0), pipeline_mode=pl.Buffered(1)),
        out_shape=jax.ShapeDtypeStruct((m, d), F32),
        scratch_shapes=[pltpu.VMEM((tm, d), BF16)],
        compiler_params=pltpu.CompilerParams(
            dimension_semantics=("arbitrary", "arbitrary"), vmem_limit_bytes=V7X_VMEM_LIMIT),
        name="ffn",
    )(x, gi, go, wg, wu, wd)


def _inproj_kernel(x_ref, g_ref, w_ref, q_ref, kb_ref, vb_ref, k_ref, v_ref, u_ref, *, q_scale, n_heads):
    tm = x_ref.shape[0]
    width = q_ref.shape[1]
    xn = _rms(x_ref[...], g_ref[...]).astype(BF16)

    def proj(c):
        return _dot(xn, w_ref[:, c * width:(c + 1) * width])

    q_ref[...] = (proj(0) * q_scale).astype(BF16)
    for c, ref, bf_ref in ((1, k_ref, kb_ref), (2, v_ref, vb_ref)):
        p = proj(c)
        for h in range(n_heads):
            ref[pl.ds(h, tm, stride=n_heads), :] = p[:, _lanes(h)]
        bf_ref[...] = p.astype(BF16)
    u_ref[...] = proj(3)


def _inproj(x, g, w_in, *, layer, tm, width):
    m, d = x.shape
    n_heads = width // HEAD_DIM
    row_blk = lambda i: (i, 0)
    wide = pl.BlockSpec((tm, width), row_blk)
    by_head = pl.BlockSpec((tm * n_heads, HEAD_DIM), row_blk)
    out_bf = jax.ShapeDtypeStruct((m, width), BF16)
    out_heads = jax.ShapeDtypeStruct((m * n_heads, HEAD_DIM), F32)
    return pl.pallas_call(
        functools.partial(_inproj_kernel, q_scale=HEAD_DIM ** -0.5, n_heads=n_heads),
        grid=(m // tm,),
        in_specs=[
            pl.BlockSpec((tm, d), row_blk),
            pl.BlockSpec((1, d), lambda i: (0, 0)),
            pl.BlockSpec((None,) + w_in.shape[1:], lambda i: (layer, 0, 0), pipeline_mode=pl.Buffered(1)),
        ],
        out_specs=[wide, wide, wide, by_head, by_head, wide],
        out_shape=[out_bf, out_bf, out_bf, out_heads, out_heads, jax.ShapeDtypeStruct((m, width), F32)],
        compiler_params=pltpu.CompilerParams(
            dimension_semantics=("arbitrary",), vmem_limit_bytes=V7X_VMEM_LIMIT),
        name="inproj",
    )(x, g, w_in)


def _lanes(h):
    return slice(h * HEAD_DIM, (h + 1) * HEAD_DIM)


def _sb_block(qs, ks, vs, tri, mask):
    return _sb_blocks(qs, [(ks, vs, mask)], tri)[0]


def _sb_blocks(qs, blocks, tri):
    tq = qs[0].shape[0]
    zs = [jnp.concatenate(
        [lax.dot_general(q, k, (((1,), (1,)), ((), ())), preferred_element_type=F32) for q, k in zip(qs, ks)],
        axis=0) for ks, _, _ in blocks]
    suffixes = []
    for z, (_, _, mask) in zip(zs, blocks):
        sp = jnp.maximum(z, 0.0) + jnp.log(1.0 + jnp.exp2(jnp.abs(z) * -LOG2_E))
        if mask is not None:
            sp = jnp.where(mask, sp, 0.0)
        tk = z.shape[1]
        suffixes.append(_dot(sp.astype(BF16), tri[:tk, :tk]))
    out = []
    for z, suffix, (_, vs, mask) in zip(zs, suffixes, blocks):
        arg = z - suffix
        if mask is not None:
            arg = jnp.where(mask, arg, -jnp.inf)
        pb = jnp.exp(arg.astype(BF16))
        pv = jnp.concatenate([_dot(pb[h * tq:(h + 1) * tq], v) for h, v in enumerate(vs)], axis=0)
        out.append((pv, suffix[:, :1]))
    return out


def _sb_accumulate(carry_ref, acc_ref, blocks):
    neg_carry = carry_ref[...]
    acc = acc_ref[...]
    for pv, row_sum in blocks:
        acc = acc + jnp.exp(neg_carry) * pv
        neg_carry = neg_carry - row_sum
    carry_ref[...] = neg_carry
    acc_ref[...] = acc


def _sb_step(qs, ks, vs, carry_ref, acc_ref, tri, mask):
    _sb_accumulate(carry_ref, acc_ref, [_sb_block(qs, ks, vs, tri, mask)])


def _attn_prompt_kernel(q_ref, k_ref, v_ref, mk_ref, mv_ref, tri_ref, _o_in_ref, o_ref, carry_ref, acc_ref,
                        meta_pv_ref, meta_sum_ref, *, t, hg):
    i = pl.program_id(2)
    tri = tri_ref[...]
    heads = range(hg)
    qs = [q_ref[:, _lanes(h)] for h in heads]
    carry_ref[...] = jnp.zeros_like(carry_ref)
    acc_ref[...] = jnp.zeros_like(acc_ref)

    def token_block(c, mask=None):
        off = pl.multiple_of(c * t, t)
        return ([k_ref[pl.ds(off, t), _lanes(h)] for h in heads],
                [v_ref[pl.ds(off, t), _lanes(h)] for h in heads], mask)

    row = lax.broadcasted_iota(jnp.int32, (hg * t, t), 0) & (t - 1)
    col = lax.broadcasted_iota(jnp.int32, (hg * t, t), 1)
    mcol = lax.broadcasted_iota(jnp.int32, (hg * t, mk_ref.shape[0]), 1)
    diag = token_block(i, col < row)
    meta = ([mk_ref[:, _lanes(h)] for h in heads], [mv_ref[:, _lanes(h)] for h in heads], mcol < N_META)

    def head_group(token_blocks):
        *done, (meta_pv, meta_sum) = _sb_blocks(qs, token_blocks + [meta], tri)
        _sb_accumulate(carry_ref, acc_ref, done)
        meta_pv_ref[...] = meta_pv
        meta_sum_ref[...] = meta_sum

    for extra in range(LOOP_BLOCKS):
        @pl.when(i % LOOP_BLOCKS == extra)
        def _():
            head_group([diag] + [token_block(i - 1 - n) for n in range(extra)])

    top = i - i % LOOP_BLOCKS

    def body(s, _):
        first = top - 1 - LOOP_BLOCKS * s
        _sb_accumulate(carry_ref, acc_ref,
                       _sb_blocks(qs, [token_block(first - n) for n in range(LOOP_BLOCKS)], tri))
        return 0

    lax.fori_loop(0, top // LOOP_BLOCKS, body, 0)
    _sb_accumulate(carry_ref, acc_ref, [(meta_pv_ref[...], meta_sum_ref[...])])
    for h in heads:
        o_ref[:, _lanes(h)] = acc_ref[h * t:(h + 1) * t, :].astype(o_ref.dtype)


def _attn_prompt(q, kb, vb, tri, o_all, *, n_batch, seq, n_heads, meta_blk, hg):
    t = tri.shape[0]
    assert t & (t - 1) == 0 and seq % t == 0 and n_heads % hg == 0
    n_q = seq // t
    w = hg * HEAD_DIM
    return pl.pallas_call(
        functools.partial(_attn_prompt_kernel, t=t, hg=hg),
        grid=(n_batch, n_heads // hg, n_q),
        in_specs=[
            pl.BlockSpec((t, w), lambda b, g, i: (b * n_q + i, g)),
            pl.BlockSpec((seq, w), lambda b, g, i: (b, g)),
            pl.BlockSpec((seq, w), lambda b, g, i: (b, g)),
            pl.BlockSpec((ROW_BLOCK, w), lambda b, g, i: (meta_blk, g)),
            pl.BlockSpec((ROW_BLOCK, w), lambda b, g, i: (meta_blk, g)),
            pl.BlockSpec(tri.shape, lambda b, g, i: (0, 0)),
            pl.BlockSpec(memory_space=pl.ANY),
        ],
        out_specs=pl.BlockSpec((t, w), lambda b, g, i: (b * n_q + i, g)),
        out_shape=jax.ShapeDtypeStruct(o_all.shape, o_all.dtype),
        input_output_aliases={6: 0},
        scratch_shapes=[pltpu.VMEM((hg * t, 1), F32), pltpu.VMEM((hg * t, HEAD_DIM), F32),
                        pltpu.VMEM((hg * t, HEAD_DIM), F32), pltpu.VMEM((hg * t, 1), F32)],
        compiler_params=pltpu.CompilerParams(
            dimension_semantics=("arbitrary", "arbitrary", "arbitrary"), vmem_limit_bytes=V7X_VMEM_LIMIT),
        name="attn_prompt",
    )(q, kb, vb, kb, vb, tri, o_all)


def _attn_meta_kernel(q_ref, k_ref, v_ref, tri_ref, _o_in_ref, o_ref, carry_ref, acc_ref, *, n_heads):
    n = q_ref.shape[0]
    carry_ref[...] = jnp.zeros_like(carry_ref)
    acc_ref[...] = jnp.zeros_like(acc_ref)
    row = lax.broadcasted_iota(jnp.int32, (n_heads * n, n), 0) & (n - 1)
    col = lax.broadcasted_iota(jnp.int32, (n_heads * n, n), 1)
    mask = (col < row) & (col < N_META)
    heads = range(n_heads)
    _sb_step([q_ref[:, _lanes(h)] for h in heads], [k_ref[:, _lanes(h)] for h in heads],
             [v_ref[:, _lanes(h)] for h in heads], carry_ref, acc_ref, tri_ref[...], mask)
    for h in heads:
        o_ref[:, _lanes(h)] = acc_ref[h * n:(h + 1) * n, :].astype(o_ref.dtype)


def _attn_meta(q, kb, vb, tri, o_all, *, n_heads, meta_blk):
    w = n_heads * HEAD_DIM
    blk = pl.BlockSpec((ROW_BLOCK, w), lambda i: (meta_blk, 0))
    return pl.pallas_call(
        functools.partial(_attn_meta_kernel, n_heads=n_heads),
        grid=(1,),
        in_specs=[blk, blk, blk, pl.BlockSpec(tri.shape, lambda i: (0, 0)), pl.BlockSpec(memory_space=pl.ANY)],
        out_specs=blk,
        out_shape=jax.ShapeDtypeStruct(o_all.shape, o_all.dtype),
        input_output_aliases={4: 0},
        scratch_shapes=[pltpu.VMEM((n_heads * ROW_BLOCK, 1), F32), pltpu.VMEM((n_heads * ROW_BLOCK, HEAD_DIM), F32)],
        compiler_params=pltpu.CompilerParams(dimension_semantics=("arbitrary",)),
        name="attn_meta",
    )(q, kb, vb, tri, o_all)


def _attn_sample_kernel(q_ref, nk_ref, nv_ref, ck_ref, cv_ref, tri_ref, _o_in_ref, o_ref, carry_ref, acc_ref,
                        *, n_heads, tk, n_chunks):
    t = pl.program_id(1)
    lq = q_ref.shape[0]
    heads = range(n_heads)
    tri = tri_ref[...]
    qs = [q_ref[:, _lanes(h)] for h in heads]

    @pl.when(t == 0)
    def _():
        carry_ref[...] = jnp.zeros_like(carry_ref)
        acc_ref[...] = jnp.zeros_like(acc_ref)
        row = lax.broadcasted_iota(jnp.int32, (n_heads * lq, lq), 0) & (lq - 1)
        col = lax.broadcasted_iota(jnp.int32, (n_heads * lq, lq), 1)
        _sb_step(qs, [nk_ref[:, _lanes(h)] for h in heads], [nv_ref[:, _lanes(h)] for h in heads],
                 carry_ref, acc_ref, tri, col < row)

    n_sub = ck_ref.shape[0] // (tk * n_heads)

    def sub_block(s):
        head_rows = lambda h: pl.ds(s * tk * n_heads + h, tk, stride=n_heads)
        return ([ck_ref[head_rows(h), :].astype(BF16) for h in heads],
                [cv_ref[head_rows(h), :].astype(BF16) for h in heads], None)

    order = list(reversed(range(n_sub)))
    results = []
    for g in range(0, n_sub, LOOP_BLOCKS):
        results += _sb_blocks(qs, [sub_block(s) for s in order[g:g + LOOP_BLOCKS]], tri)
    _sb_accumulate(carry_ref, acc_ref, results)

    @pl.when(t == n_chunks - 1)
    def _():
        for h in heads:
            o_ref[:, _lanes(h)] = acc_ref[h * lq:(h + 1) * lq, :].astype(o_ref.dtype)


def _attn_sample(q, kb, vb, cache_k, cache_v, tri, o_all, *, layer, lq, row0, chunk):
    _, n_batch, past, n_heads, _ = cache_k.shape
    assert lq & (lq - 1) == 0
    tk = tri.shape[0]
    width = n_heads * HEAD_DIM
    n_chunks = past // chunk
    blk0 = row0 // lq
    new_blk = pl.BlockSpec((lq, width), lambda b, t: (blk0 + b, 0))
    cache_rows = lambda c: c.reshape(-1, HEAD_DIM)
    cache_blk = pl.BlockSpec((chunk * n_heads, HEAD_DIM),
                             lambda b, t: ((layer * n_batch + b) * n_chunks + n_chunks - 1 - t, 0))
    return pl.pallas_call(
        functools.partial(_attn_sample_kernel, n_heads=n_heads, tk=tk, n_chunks=n_chunks),
        grid=(n_batch, n_chunks),
        in_specs=[new_blk, new_blk, new_blk, cache_blk, cache_blk, pl.BlockSpec(tri.shape, lambda b, t: (0, 0)),
                  pl.BlockSpec(memory_space=pl.ANY)],
        out_specs=new_blk,
        out_shape=jax.ShapeDtypeStruct(o_all.shape, o_all.dtype),
        input_output_aliases={6: 0},
        scratch_shapes=[pltpu.VMEM((n_heads * lq, 1), F32), pltpu.VMEM((n_heads * lq, HEAD_DIM), F32)],
        compiler_params=pltpu.CompilerParams(
            dimension_semantics=("arbitrary", "arbitrary"), vmem_limit_bytes=V7X_VMEM_LIMIT),
        name="attn_sample",
    )(q, kb, vb, cache_rows(cache_k), cache_rows(cache_v), tri, o_all)


def _mixout_kernel(osb_ref, u_ref, halo_ref, h_ref, wp_ref, ps_ref, wo_ref, g_ref, o_ref, ext_ref, cat_ref,
                   *, meta_seq, sb_width):
    n_seq, lt, pw = u_ref.shape
    cg = pw // len(POOL_WINDOWS)
    rows = n_seq * lt
    ext_ref[:, :HALO, :] = halo_ref[...]
    ext_ref[:, HALO:, :] = u_ref[...]
    cat_ref[:, :sb_width] = osb_ref[...]

    seq_id = pl.program_id(0) * n_seq + lax.broadcasted_iota(jnp.int32, (n_seq, 1, 1), 0)
    n_hist = jnp.where(seq_id >= meta_seq, 0, POOL_HIST)
    n_before = (n_hist + lax.broadcasted_iota(jnp.int32, (n_seq, lt, 1), 1)).astype(F32)

    for g, w in enumerate(POOL_WINDOWS):
        lanes = pl.ds(g * cg, cg)
        cur = ext_ref[:, pl.ds(HALO, lt), lanes]
        s = cur
        for back in range(1, w):
            s = s + ext_ref[:, pl.ds(HALO - back, lt), lanes]
        cnt = jnp.minimum(float(w), n_before + 1.0)
        dlt = (s / cnt - cur).reshape(rows, cg).astype(BF16)
        y = _dot(dlt, wp_ref[g]) * ps_ref[:, lanes]
        cat_ref[:, pl.ds(sb_width + g * cg, cg)] = y.astype(BF16)

    mixed = _dot(cat_ref[...], wo_ref[...])
    o_ref[...] = h_ref[...] + _rms(mixed, g_ref[...])


def _mixout(o_sb, u, halo, h, w_pool, pool_scale, w_out, g, *, layer, tm):
    m, d = h.shape
    pw = u.shape[1]
    sb_width = o_sb.shape[1]
    seq_per = tm // SEQ_TILE
    u3 = u.reshape(m // SEQ_TILE, SEQ_TILE, pw)
    return pl.pallas_call(
        functools.partial(_mixout_kernel, meta_seq=(m - ROW_BLOCK) // SEQ_TILE, sb_width=sb_width),
        grid=(m // tm,),
        in_specs=[
            pl.BlockSpec((tm, sb_width), lambda i: (i, 0)),
            pl.BlockSpec((seq_per, SEQ_TILE, pw), lambda i: (i, 0, 0)),
            pl.BlockSpec((seq_per, HALO, pw), lambda i: (i, 0, 0)),
            pl.BlockSpec((tm, d), lambda i: (i, 0)),
            pl.BlockSpec((None,) + w_pool.shape[1:], lambda i: (layer, 0, 0, 0)),
            pl.BlockSpec((1, pw), lambda i: (0, 0)),
            pl.BlockSpec((None,) + w_out.shape[1:], lambda i: (layer, 0, 0)),
            pl.BlockSpec((1, d), lambda i: (0, 0)),
        ],
        out_specs=pl.BlockSpec((tm, d), lambda i: (i, 0)),
        out_shape=jax.ShapeDtypeStruct((m, d), F32),
        scratch_shapes=[pltpu.VMEM((seq_per, HALO + SEQ_TILE, pw), F32),
                        pltpu.VMEM((tm, sb_width + pw), BF16)],
        compiler_params=pltpu.CompilerParams(
            dimension_semantics=("arbitrary",), vmem_limit_bytes=V7X_VMEM_LIMIT),
        name="mixout",
    )(o_sb, u3, halo, h, w_pool, pool_scale, w_out, g)


def _row_tile(m, cap):
    blocks = m // ROW_BLOCK
    best = 1
    for dvs in range(1, blocks + 1):
        if blocks % dvs == 0 and dvs * ROW_BLOCK <= cap:
            best = dvs
    return best * ROW_BLOCK


def _col_tile(n, cap):
    best = HEAD_DIM
    for t in range(HEAD_DIM, cap + 1, HEAD_DIM):
        if n % t == 0:
            best = t
    return best


def kernel(x_prompt, x_sample, cache_k, cache_v, state_pool, meta_tokens, w_in, w_out, w_pool, pool_scale,
           norm_gains, ffn1_gate, ffn1_up, ffn1_down, ffn2_gate, ffn2_up, ffn2_down):
    n_b, seq, d = x_prompt.shape
    s_b, s_len, _ = x_sample.shape
    depth, _, past, n_heads, head_dim = cache_k.shape
    pw = state_pool.shape[-1]
    sbw = n_heads * head_dim
    assert head_dim == HEAD_DIM and meta_tokens.shape[0] == N_META and sbw == pw
    assert w_in.shape[-1] == 3 * sbw + pw and state_pool.shape[2] == POOL_HIST
    assert seq % SEQ_TILE == 0 and s_len == SEQ_TILE and seq >= POOL_HIST and N_META == HALO

    r_p = n_b * seq
    r_t = r_p + s_b * s_len
    assert r_t % ROW_BLOCK == 0
    m = r_t + ROW_BLOCK
    meta_blk = r_t // ROW_BLOCK

    tk = 256
    assert seq % tk == 0 and past % tk == 0
    chunk = next(c for c in (2048, 1024, tk) if past % c == 0)
    hg = next(g for g in (8, 4, 2, 1) if n_heads % g == 0)
    tri = (lax.broadcasted_iota(jnp.int32, (tk, tk), 0) >= lax.broadcasted_iota(jnp.int32, (tk, tk), 1)).astype(BF16)

    x = jnp.concatenate([
        x_prompt.reshape(r_p, d), x_sample.reshape(s_b * s_len, d), meta_tokens.astype(F32),
        jnp.zeros((ROW_BLOCK - N_META, d), F32)], axis=0)

    tm_ffn = _row_tile(m, 1152)
    tf = _col_tile(ffn1_gate.shape[-1], 512)
    tm_proj = _row_tile(m, 576)
    n_tiles_p = seq // SEQ_TILE

    w_in_bf, w_out_bf, w_pool_bf = (w.astype(BF16) for w in (w_in, w_out, w_pool))
    seq_of = lambda row: row // SEQ_TILE
    outs = [[] for _ in range(6)]
    for l in range(depth):
        gains = norm_gains[l].astype(F32)
        gain = lambda n: gains[n][None, :]

        x = _ffn(x, gain(0), gain(1), ffn1_gate, ffn1_up, ffn1_down, layer=l, tm=tm_ffn, tf=tf)
        q, kb, vb, k, v, u = _inproj(x, gain(2), w_in_bf, layer=l, tm=tm_proj, width=sbw)

        o_sb = jnp.zeros((m, sbw), BF16)
        o_sb = _attn_prompt(q, kb, vb, tri, o_sb, n_batch=n_b, seq=seq, n_heads=n_heads, meta_blk=meta_blk, hg=hg)
        o_sb = _attn_sample(q, kb, vb, cache_k, cache_v, tri, o_sb, layer=l, lq=s_len, row0=r_p, chunk=chunk)
        o_sb = _attn_meta(q, kb, vb, tri, o_sb, n_heads=n_heads, meta_blk=meta_blk)

        tails = u.reshape(seq_of(m), SEQ_TILE, pw)[:, SEQ_TILE - HALO:]
        halo_p = jnp.concatenate(
            [part for b in range(n_b)
             for part in (u[r_t:r_t + N_META][None], tails[b * n_tiles_p:(b + 1) * n_tiles_p - 1])], axis=0)
        halo_s = jnp.pad(state_pool[l].astype(F32), ((0, 0), (HALO - POOL_HIST, 0), (0, 0)))
        halo = jnp.concatenate([halo_p, halo_s, jnp.zeros((ROW_BLOCK // SEQ_TILE, HALO, pw), F32)], axis=0)

        x = _mixout(o_sb, u, halo, x, w_pool_bf, pool_scale[l].astype(F32)[None, :], w_out_bf, gain(3),
                    layer=l, tm=tm_proj)
        x = _ffn(x, gain(4), gain(5), ffn2_gate, ffn2_up, ffn2_down, layer=l, tm=tm_ffn, tf=tf)

        k = k.reshape(m, n_heads, HEAD_DIM)
        v = v.reshape(m, n_heads, HEAD_DIM)
        for a, pieces in ((k, outs[0]), (v, outs[1])):
            for b in range(n_b):
                pieces += [a[r_t:r_t + N_META], a[b * seq:(b + 1) * seq]]
        outs[2].append(jnp.stack([u[(b + 1) * seq - POOL_HIST:(b + 1) * seq] for b in range(n_b)]))
        outs[3].append(k[r_p:r_t].reshape(s_b, s_len, n_heads, HEAD_DIM))
        outs[4].append(v[r_p:r_t].reshape(s_b, s_len, n_heads, HEAD_DIM))
        outs[5].append(tails[seq_of(r_p):seq_of(r_t), HALO - POOL_HIST:])

    y_prompt = x[:r_p].reshape(n_b, seq, d)
    y_sample = x[r_p:r_t].reshape(s_b, s_len, d)
    prompt_cache = lambda pieces: jnp.concatenate(pieces, axis=0).reshape(
        depth, n_b, N_META + seq, n_heads, HEAD_DIM)
    return (y_prompt, y_sample, prompt_cache(outs[0]), prompt_cache(outs[1])) + tuple(
        jnp.stack(o) for o in outs[2:])
```

```python
import functools

import jax
import jax.numpy as jnp
from jax import lax
from jax.experimental import pallas as pl
from jax.experimental.pallas import tpu as pltpu

F32 = jnp.float32
BF16 = jnp.bfloat16

RMS_EPS = 1e-6
LOG2_E = 1.4426950408889634
HEAD_DIM = 128
MXU_WIDTH = 256
N_META = 16
POOL_WINDOWS = (2, 4, 8, 16)
POOL_HIST = max(POOL_WINDOWS) - 1
HALO = 16
SEQ_TILE = 64
ROW_BLOCK = 128
LOOP_BLOCKS = 2
V7X_VMEM_BYTES = 64 * 1024 * 1024
V7X_VMEM_LIMIT = V7X_VMEM_BYTES - 4 * 1024 * 1024


def _rms(x, g):
    ms = jnp.mean(x * x, axis=-1, keepdims=True)
    return x * lax.rsqrt(ms + RMS_EPS) * g


def _dot(a, b):
    return jnp.dot(a, b, preferred_element_type=F32)


def _ffn_kernel(x_hbm, gi_ref, go_ref, wg_ref, wu_ref, wd_ref, o_hbm, x_ref, o_ref, xn_ref, sem_in, sem_out,
                *, n_f):
    i = pl.program_id(0)
    j = pl.program_id(1)
    tm = x_ref.shape[0]
    n_rb = tm // ROW_BLOCK

    def rows_of(r):
        return pl.ds(r * ROW_BLOCK, ROW_BLOCK)

    def x_copy(r):
        return pltpu.make_async_copy(x_hbm.at[pl.ds(i * tm + r * ROW_BLOCK, ROW_BLOCK), :],
                                     x_ref.at[rows_of(r), :], sem_in.at[r])

    def o_copy(tile, r):
        return pltpu.make_async_copy(o_ref.at[rows_of(r), :],
                                     o_hbm.at[pl.ds(tile * tm + r * ROW_BLOCK, ROW_BLOCK), :], sem_out.at[r])

    @pl.when(j == 0)
    def _():
        for r in range(n_rb):
            x_copy(r).start()
        gi = gi_ref[...]
        for r in range(n_rb):
            x_copy(r).wait()

            @pl.when(i > 0)
            def _():
                o_copy(i - 1, r).wait()

            xn_ref[rows_of(r), :] = _rms(x_ref[rows_of(r), :], gi).astype(BF16)
            o_ref[rows_of(r), :] = jnp.zeros((ROW_BLOCK, o_ref.shape[1]), F32)

    xn = xn_ref[...]
    tf = wg_ref.shape[1]
    hidden = []
    for c in range(0, tf, MXU_WIDTH):
        g = _dot(xn, wg_ref[:, c:c + MXU_WIDTH].astype(BF16))
        u = _dot(xn, wu_ref[:, c:c + MXU_WIDTH].astype(BF16))
        hidden.append((g * jax.nn.sigmoid(g) * u).astype(BF16))
    hidden = jnp.concatenate(hidden, axis=1)
    for c in range(0, o_ref.shape[1], MXU_WIDTH):
        o_ref[:, c:c + MXU_WIDTH] += _dot(hidden, wd_ref[:, c:c + MXU_WIDTH].astype(BF16))

    @pl.when(j == n_f - 1)
    def _():
        go = go_ref[...]
        for r in range(n_rb):
            o_ref[rows_of(r), :] = x_ref[rows_of(r), :] + 0.5 * _rms(o_ref[rows_of(r), :], go)
            o_copy(i, r).start()

        @pl.when(i == pl.num_programs(0) - 1)
        def _():
            for r in range(n_rb):
                o_copy(i, r).wait()


def _ffn(x, gi, go, wg, wu, wd, *, layer, tm, tf):
    m, d = x.shape
    f = wg.shape[2]
    n_f = f // tf
    n_rb = tm // ROW_BLOCK
    return pl.pallas_call(
        functools.partial(_ffn_kernel, n_f=n_f),
        grid=(m // tm, n_f),
        in_specs=[
            pl.BlockSpec(memory_space=pl.ANY),
            pl.BlockSpec((1, d), lambda i, j: (0, 0)),
            pl.BlockSpec((1, d), lambda i, j: (0, 0)),
            pl.BlockSpec((None, d, tf), lambda i, j: (layer, 0, j)),
            pl.BlockSpec((None, d, tf), lambda i, j: (layer, 0, j)),
            pl.BlockSpec((None, tf, d), lambda i, j: (layer, j, 0)),
        ],
        out_specs=pl.BlockSpec(memory_space=pl.ANY),
        out_shape=jax.ShapeDtypeStruct((m, d), F32),
        scratch_shapes=[pltpu.VMEM((tm, d), F32), pltpu.VMEM((tm, d), F32), pltpu.VMEM((tm, d), BF16),
                        pltpu.SemaphoreType.DMA((n_rb,)), pltpu.SemaphoreType.DMA((n_rb,))],
        compiler_params=pltpu.CompilerParams(
            dimension_semantics=("arbitrary", "arbitrary"), vmem_limit_bytes=V7X_VMEM_LIMIT),
        name="ffn",
    )(x, gi, go, wg, wu, wd)


def _inproj_kernel(x_ref, g_ref, w_ref, q_ref, kb_ref, vb_ref, k_ref, v_ref, u_ref, *, q_scale, n_heads):
    tm = x_ref.shape[0]
    width = q_ref.shape[1]
    xn = _rms(x_ref[...], g_ref[...]).astype(BF16)

    def proj(c):
        return _dot(xn, w_ref[:, c * width:(c + 1) * width])

    q_ref[...] = (proj(0) * q_scale).astype(BF16)
    for c, ref, bf_ref in ((1, k_ref, kb_ref), (2, v_ref, vb_ref)):
        p = proj(c)
        for h in range(n_heads):
            ref[pl.ds(h, tm, stride=n_heads), :] = p[:, _lanes(h)]
        bf_ref[...] = p.astype(BF16)
    u_ref[...] = proj(3)


def _inproj(x, g, w_in, *, layer, tm, width):
    m, d = x.shape
    n_heads = width // HEAD_DIM
    row_blk = lambda i: (i, 0)
    wide = pl.BlockSpec((tm, width), row_blk)
    by_head = pl.BlockSpec((tm * n_heads, HEAD_DIM), row_blk)
    out_bf = jax.ShapeDtypeStruct((m, width), BF16)
    out_heads = jax.ShapeDtypeStruct((m * n_heads, HEAD_DIM), F32)
    return pl.pallas_call(
        functools.partial(_inproj_kernel, q_scale=HEAD_DIM ** -0.5, n_heads=n_heads),
        grid=(m // tm,),
        in_specs=[
            pl.BlockSpec((tm, d), row_blk),
            pl.BlockSpec((1, d), lambda i: (0, 0)),
            pl.BlockSpec((None,) + w_in.shape[1:], lambda i: (layer, 0, 0), pipeline_mode=pl.Buffered(1)),
        ],
        out_specs=[wide, wide, wide, by_head, by_head, wide],
        out_shape=[out_bf, out_bf, out_bf, out_heads, out_heads, jax.ShapeDtypeStruct((m, width), F32)],
        compiler_params=pltpu.CompilerParams(
            dimension_semantics=("arbitrary",), vmem_limit_bytes=V7X_VMEM_LIMIT),
        name="inproj",
    )(x, g, w_in)


def _lanes(h):
    return slice(h * HEAD_DIM, (h + 1) * HEAD_DIM)


def _sb_block(qs, ks, vs, tri, mask):
    return _sb_blocks(qs, [(ks, vs, mask)], tri)[0]


def _sb_blocks(qs, blocks, tri):
    tq = qs[0].shape[0]
    zs = [jnp.concatenate(
        [lax.dot_general(q, k, (((1,), (1,)), ((), ())), preferred_element_type=F32) for q, k in zip(qs, ks)],
        axis=0) for ks, _, _ in blocks]
    suffixes = []
    for z, (_, _, mask) in zip(zs, blocks):
        sp = jnp.maximum(z, 0.0) + jnp.log(1.0 + jnp.exp2(jnp.abs(z) * -LOG2_E))
        if mask is not None:
            sp = jnp.where(mask, sp, 0.0)
        tk = z.shape[1]
        suffixes.append(_dot(sp.astype(BF16), tri[:tk, :tk]))
    out = []
    for z, suffix, (_, vs, mask) in zip(zs, suffixes, blocks):
        arg = z - suffix
        if mask is not None:
            arg = jnp.where(mask, arg, -jnp.inf)
        pb = jnp.exp(arg.astype(BF16))
        pv = jnp.concatenate([_dot(pb[h * tq:(h + 1) * tq], v) for h, v in enumerate(vs)], axis=0)
        out.append((pv, suffix[:, :1]))
    return out


def _sb_accumulate(carry_ref, acc_ref, blocks):
    neg_carry = carry_ref[...]
    acc = acc_ref[...]
    for pv, row_sum in blocks:
        acc = acc + jnp.exp(neg_carry) * pv
        neg_carry = neg_carry - row_sum
    carry_ref[...] = neg_carry
    acc_ref[...] = acc


def _sb_step(qs, ks, vs, carry_ref, acc_ref, tri, mask):
    _sb_accumulate(carry_ref, acc_ref, [_sb_block(qs, ks, vs, tri, mask)])


def _attn_prompt_kernel(q_ref, k_ref, v_ref, mk_ref, mv_ref, tri_ref, _o_in_ref, o_ref, carry_ref, acc_ref,
                        meta_pv_ref, meta_sum_ref, *, t, hg):
    i = pl.program_id(2)
    tri = tri_ref[...]
    heads = range(hg)
    qs = [q_ref[:, _lanes(h)] for h in heads]
    carry_ref[...] = jnp.zeros_like(carry_ref)
    acc_ref[...] = jnp.zeros_like(acc_ref)

    def token_block(c, mask=None):
        off = pl.multiple_of(c * t, t)
        return ([k_ref[pl.ds(off, t), _lanes(h)] for h in heads],
                [v_ref[pl.ds(off, t), _lanes(h)] for h in heads], mask)

    row = lax.broadcasted_iota(jnp.int32, (hg * t, t), 0) & (t - 1)
    col = lax.broadcasted_iota(jnp.int32, (hg * t, t), 1)
    mcol = lax.broadcasted_iota(jnp.int32, (hg * t, mk_ref.shape[0]), 1)
    diag = token_block(i, col < row)
    meta = ([mk_ref[:, _lanes(h)] for h in heads], [mv_ref[:, _lanes(h)] for h in heads], mcol < N_META)

    def head_group(token_blocks):
        *done, (meta_pv, meta_sum) = _sb_blocks(qs, token_blocks + [meta], tri)
        _sb_accumulate(carry_ref, acc_ref, done)
        meta_pv_ref[...] = meta_pv
        meta_sum_ref[...] = meta_sum

    for extra in range(LOOP_BLOCKS):
        @pl.when(i % LOOP_BLOCKS == extra)
        def _():
            head_group([diag] + [token_block(i - 1 - n) for n in range(extra)])

    top = i - i % LOOP_BLOCKS

    def body(s, _):
        first = top - 1 - LOOP_BLOCKS * s
        _sb_accumulate(carry_ref, acc_ref,
                       _sb_blocks(qs, [token_block(first - n) for n in range(LOOP_BLOCKS)], tri))
        return 0

    lax.fori_loop(0, top // LOOP_BLOCKS, body, 0)
    _sb_accumulate(carry_ref, acc_ref, [(meta_pv_ref[...], meta_sum_ref[...])])
    for h in heads:
        o_ref[:, _lanes(h)] = acc_ref[h * t:(h + 1) * t, :].astype(o_ref.dtype)


def _attn_prompt(q, kb, vb, tri, o_all, *, n_batch, seq, n_heads, meta_blk, hg):
    t = tri.shape[0]
    assert t & (t - 1) == 0 and seq % t == 0 and n_heads % hg == 0
    n_q = seq // t
    w = hg * HEAD_DIM
    return pl.pallas_call(
        functools.partial(_attn_prompt_kernel, t=t, hg=hg),
        grid=(n_batch, n_heads // hg, n_q),
        in_specs=[
            pl.BlockSpec((t, w), lambda b, g, i: (b * n_q + i, g)),
            pl.BlockSpec((seq, w), lambda b, g, i: (b, g)),
            pl.BlockSpec((seq, w), lambda b, g, i: (b, g)),
            pl.BlockSpec((ROW_BLOCK, w), lambda b, g, i: (meta_blk, g)),
            pl.BlockSpec((ROW_BLOCK, w), lambda b, g, i: (meta_blk, g)),
            pl.BlockSpec(tri.shape, lambda b, g, i: (0, 0)),
            pl.BlockSpec(memory_space=pl.ANY),
        ],
        out_specs=pl.BlockSpec((t, w), lambda b, g, i: (b * n_q + i, g)),
        out_shape=jax.ShapeDtypeStruct(o_all.shape, o_all.dtype),
        input_output_aliases={6: 0},
        scratch_shapes=[pltpu.VMEM((hg * t, 1), F32), pltpu.VMEM((hg * t, HEAD_DIM), F32),
                        pltpu.VMEM((hg * t, HEAD_DIM), F32), pltpu.VMEM((hg * t, 1), F32)],
        compiler_params=pltpu.CompilerParams(
            dimension_semantics=("arbitrary", "arbitrary", "arbitrary"), vmem_limit_bytes=V7X_VMEM_LIMIT),
        name="attn_prompt",
    )(q, kb, vb, kb, vb, tri, o_all)


def _attn_meta_kernel(q_ref, k_ref, v_ref, tri_ref, _o_in_ref, o_ref, carry_ref, acc_ref, *, n_heads):
    n = q_ref.shape[0]
    carry_ref[...] = jnp.zeros_like(carry_ref)
    acc_ref[...] = jnp.zeros_like(acc_ref)
    row = lax.broadcasted_iota(jnp.int32, (n_heads * n, n), 0) & (n - 1)
    col = lax.broadcasted_iota(jnp.int32, (n_heads * n, n), 1)
    mask = (col < row) & (col < N_META)
    heads = range(n_heads)
    _sb_step([q_ref[:, _lanes(h)] for h in heads], [k_ref[:, _lanes(h)] for h in heads],
             [v_ref[:, _lanes(h)] for h in heads], carry_ref, acc_ref, tri_ref[...], mask)
    for h in heads:
        o_ref[:, _lanes(h)] = acc_ref[h * n:(h + 1) * n, :].astype(o_ref.dtype)


def _attn_meta(q, kb, vb, tri, o_all, *, n_heads, meta_blk):
    w = n_heads * HEAD_DIM
    blk = pl.BlockSpec((ROW_BLOCK, w), lambda i: (meta_blk, 0))
    return pl.pallas_call(
        functools.partial(_attn_meta_kernel, n_heads=n_heads),
        grid=(1,),
        in_specs=[blk, blk, blk, pl.BlockSpec(tri.shape, lambda i: (0, 0)), pl.BlockSpec(memory_space=pl.ANY)],
        out_specs=blk,
        out_shape=jax.ShapeDtypeStruct(o_all.shape, o_all.dtype),
        input_output_aliases={4: 0},
        scratch_shapes=[pltpu.VMEM((n_heads * ROW_BLOCK, 1), F32), pltpu.VMEM((n_heads * ROW_BLOCK, HEAD_DIM), F32)],
        compiler_params=pltpu.CompilerParams(dimension_semantics=("arbitrary",)),
        name="attn_meta",
    )(q, kb, vb, tri, o_all)


def _attn_sample_kernel(q_ref, nk_ref, nv_ref, ck_ref, cv_ref, tri_ref, _o_in_ref, o_ref, carry_ref, acc_ref,
                        *, n_heads, tk, n_chunks):
    t = pl.program_id(1)
    lq = q_ref.shape[0]
    heads = range(n_heads)
    tri = tri_ref[...]
    qs = [q_ref[:, _lanes(h)] for h in heads]

    @pl.when(t == 0)
    def _():
        carry_ref[...] = jnp.zeros_like(carry_ref)
        acc_ref[...] = jnp.zeros_like(acc_ref)
        row = lax.broadcasted_iota(jnp.int32, (n_heads * lq, lq), 0) & (lq - 1)
        col = lax.broadcasted_iota(jnp.int32, (n_heads * lq, lq), 1)
        _sb_step(qs, [nk_ref[:, _lanes(h)] for h in heads], [nv_ref[:, _lanes(h)] for h in heads],
                 carry_ref, acc_ref, tri, col < row)

    n_sub = ck_ref.shape[0] // (tk * n_heads)

    def sub_block(s):
        head_rows = lambda h: pl.ds(s * tk * n_heads + h, tk, stride=n_heads)
        return ([ck_ref[head_rows(h), :].astype(BF16) for h in heads],
                [cv_ref[head_rows(h), :].astype(BF16) for h in heads], None)

    order = list(reversed(range(n_sub)))
    results = []
    for g in range(0, n_sub, LOOP_BLOCKS):
        results += _sb_blocks(qs, [sub_block(s) for s in order[g:g + LOOP_BLOCKS]], tri)
    _sb_accumulate(carry_ref, acc_ref, results)

    @pl.when(t == n_chunks - 1)
    def _():
        for h in heads:
            o_ref[:, _lanes(h)] = acc_ref[h * lq:(h + 1) * lq, :].astype(o_ref.dtype)


def _attn_sample(q, kb, vb, cache_k, cache_v, tri, o_all, *, layer, lq, row0, chunk):
    _, n_batch, past, n_heads, _ = cache_k.shape
    assert lq & (lq - 1) == 0
    tk = tri.shape[0]
    width = n_heads * HEAD_DIM
    n_chunks = past // chunk
    blk0 = row0 // lq
    new_blk = pl.BlockSpec((lq, width), lambda b, t: (blk0 + b, 0))
    cache_rows = lambda c: c.reshape(-1, HEAD_DIM)
    cache_blk = pl.BlockSpec((chunk * n_heads, HEAD_DIM),
                             lambda b, t: ((layer * n_batch + b) * n_chunks + n_chunks - 1 - t, 0))
    return pl.pallas_call(
        functools.partial(_attn_sample_kernel, n_heads=n_heads, tk=tk, n_chunks=n_chunks),
        grid=(n_batch, n_chunks),
        in_specs=[new_blk, new_blk, new_blk, cache_blk, cache_blk, pl.BlockSpec(tri.shape, lambda b, t: (0, 0)),
                  pl.BlockSpec(memory_space=pl.ANY)],
        out_specs=new_blk,
        out_shape=jax.ShapeDtypeStruct(o_all.shape, o_all.dtype),
        input_output_aliases={6: 0},
        scratch_shapes=[pltpu.VMEM((n_heads * lq, 1), F32), pltpu.VMEM((n_heads * lq, HEAD_DIM), F32)],
        compiler_params=pltpu.CompilerParams(
            dimension_semantics=("arbitrary", "arbitrary"), vmem_limit_bytes=V7X_VMEM_LIMIT),
        name="attn_sample",
    )(q, kb, vb, cache_rows(cache_k), cache_rows(cache_v), tri, o_all)


def _mixout_kernel(osb_ref, u_ref, halo_ref, h_ref, wp_ref, ps_ref, wo_ref, g_ref, o_ref, ext_ref, cat_ref,
                   *, meta_seq, sb_width):
    n_seq, lt, pw = u_ref.shape
    cg = pw // len(POOL_WINDOWS)
    rows = n_seq * lt
    ext_ref[:, :HALO, :] = halo_ref[...]
    ext_ref[:, HALO:, :] = u_ref[...]
    cat_ref[:, :sb_width] = osb_ref[...]

    seq_id = pl.program_id(0) * n_seq + lax.broadcasted_iota(jnp.int32, (n_seq, 1, 1), 0)
    n_hist = jnp.where(seq_id >= meta_seq, 0, POOL_HIST)
    n_before = (n_hist + lax.broadcasted_iota(jnp.int32, (n_seq, lt, 1), 1)).astype(F32)

    for g, w in enumerate(POOL_WINDOWS):
        lanes = pl.ds(g * cg, cg)
        cur = ext_ref[:, pl.ds(HALO, lt), lanes]
        s = cur
        for back in range(1, w):
            s = s + ext_ref[:, pl.ds(HALO - back, lt), lanes]
        cnt = jnp.minimum(float(w), n_before + 1.0)
        dlt = (s / cnt - cur).reshape(rows, cg).astype(BF16)
        y = _dot(dlt, wp_ref[g]) * ps_ref[:, lanes]
        cat_ref[:, pl.ds(sb_width + g * cg, cg)] = y.astype(BF16)

    mixed = _dot(cat_ref[...], wo_ref[...])
    o_ref[...] = h_ref[...] + _rms(mixed, g_ref[...])


def _mixout(o_sb, u, halo, h, w_pool, pool_scale, w_out, g, *, layer, tm):
    m, d = h.shape
    pw = u.shape[1]
    sb_width = o_sb.shape[1]
    seq_per = tm // SEQ_TILE
    u3 = u.reshape(m // SEQ_TILE, SEQ_TILE, pw)
    return pl.pallas_call(
        functools.partial(_mixout_kernel, meta_seq=(m - ROW_BLOCK) // SEQ_TILE, sb_width=sb_width),
        grid=(m // tm,),
        in_specs=[
            pl.BlockSpec((tm, sb_width), lambda i: (i, 0)),
            pl.BlockSpec((seq_per, SEQ_TILE, pw), lambda i: (i, 0, 0)),
            pl.BlockSpec((seq_per, HALO, pw), lambda i: (i, 0, 0)),
            pl.BlockSpec((tm, d), lambda i: (i, 0)),
            pl.BlockSpec((None,) + w_pool.shape[1:], lambda i: (layer, 0, 0, 0)),
            pl.BlockSpec((1, pw), lambda i: (0, 0)),
            pl.BlockSpec((None,) + w_out.shape[1:], lambda i: (layer, 0, 0)),
            pl.BlockSpec((1, d), lambda i: (0, 0)),
        ],
        out_specs=pl.BlockSpec((tm, d), lambda i: (i, 0)),
        out_shape=jax.ShapeDtypeStruct((m, d), F32),
        scratch_shapes=[pltpu.VMEM((seq_per, HALO + SEQ_TILE, pw), F32),
                        pltpu.VMEM((tm, sb_width + pw), BF16)],
        compiler_params=pltpu.CompilerParams(
            dimension_semantics=("arbitrary",), vmem_limit_bytes=V7X_VMEM_LIMIT),
        name="mixout",
    )(o_sb, u3, halo, h, w_pool, pool_scale, w_out, g)


def _row_tile(m, cap):
    blocks = m // ROW_BLOCK
    best = 1
    for dvs in range(1, blocks + 1):
        if blocks % dvs == 0 and dvs * ROW_BLOCK <= cap:
            best = dvs
    return best * ROW_BLOCK


def _col_tile(n, cap):
    best = HEAD_DIM
    for t in range(HEAD_DIM, cap + 1, HEAD_DIM):
        if n % t == 0:
            best = t
    return best


def kernel(x_prompt, x_sample, cache_k, cache_v, state_pool, meta_tokens, w_in, w_out, w_pool, pool_scale,
           norm_gains, ffn1_gate, ffn1_up, ffn1_down, ffn2_gate, ffn2_up, ffn2_down):
    n_b, seq, d = x_prompt.shape
    s_b, s_len, _ = x_sample.shape
    depth, _, past, n_heads, head_dim = cache_k.shape
    pw = state_pool.shape[-1]
    sbw = n_heads * head_dim
    assert head_dim == HEAD_DIM and meta_tokens.shape[0] == N_META and sbw == pw
    assert w_in.shape[-1] == 3 * sbw + pw and state_pool.shape[2] == POOL_HIST
    assert seq % SEQ_TILE == 0 and s_len == SEQ_TILE and seq >= POOL_HIST and N_META == HALO

    r_p = n_b * seq
    r_t = r_p + s_b * s_len
    assert r_t % ROW_BLOCK == 0
    m = r_t + ROW_BLOCK
    meta_blk = r_t // ROW_BLOCK

    tk = 256
    assert seq % tk == 0 and past % tk == 0
    chunk = next(c for c in (2048, 1024, tk) if past % c == 0)
    hg = next(g for g in (8, 4, 2, 1) if n_heads % g == 0)
    tri = (lax.broadcasted_iota(jnp.int32, (tk, tk), 0) >= lax.broadcasted_iota(jnp.int32, (tk, tk), 1)).astype(BF16)

    x = jnp.concatenate([
        x_prompt.reshape(r_p, d), x_sample.reshape(s_b * s_len, d), meta_tokens.astype(F32),
        jnp.zeros((ROW_BLOCK - N_META, d), F32)], axis=0)

    tm_ffn = _row_tile(m, 1152)
    tf = _col_tile(ffn1_gate.shape[-1], 512)
    tm_proj = _row_tile(m, 576)
    n_tiles_p = seq // SEQ_TILE

    w_in_bf, w_out_bf, w_pool_bf = (w.astype(BF16) for w in (w_in, w_out, w_pool))
    seq_of = lambda row: row // SEQ_TILE
    outs = [[] for _ in range(6)]
    for l in range(depth):
        gains = norm_gains[l].astype(F32)
        gain = lambda n: gains[n][None, :]

        x = _ffn(x, gain(0), gain(1), ffn1_gate, ffn1_up, ffn1_down, layer=l, tm=tm_ffn, tf=tf)
        q, kb, vb, k, v, u = _inproj(x, gain(2), w_in_bf, layer=l, tm=tm_proj, width=sbw)

        o_sb = jnp.zeros((m, sbw), BF16)
        o_sb = _attn_prompt(q, kb, vb, tri, o_sb, n_batch=n_b, seq=seq, n_heads=n_heads, meta_blk=meta_blk, hg=hg)
        o_sb = _attn_sample(q, kb, vb, cache_k, cache_v, tri, o_sb, layer=l, lq=s_len, row0=r_p, chunk=chunk)
        o_sb = _attn_meta(q, kb, vb, tri, o_sb, n_heads=n_heads, meta_blk=meta_blk)

        tails = u.reshape(seq_of(m), SEQ_TILE, pw)[:, SEQ_TILE - HALO:]
        halo_p = jnp.concatenate(
            [part for b in range(n_b)
             for part in (u[r_t:r_t + N_META][None], tails[b * n_tiles_p:(b + 1) * n_tiles_p - 1])], axis=0)
        halo_s = jnp.pad(state_pool[l].astype(F32), ((0, 0), (HALO - POOL_HIST, 0), (0, 0)))
        halo = jnp.concatenate([halo_p, halo_s, jnp.zeros((ROW_BLOCK // SEQ_TILE, HALO, pw), F32)], axis=0)

        x = _mixout(o_sb, u, halo, x, w_pool_bf, pool_scale[l].astype(F32)[None, :], w_out_bf, gain(3),
                    layer=l, tm=tm_proj)
        x = _ffn(x, gain(4), gain(5), ffn2_gate, ffn2_up, ffn2_down, layer=l, tm=tm_ffn, tf=tf)

        k = k.reshape(m, n_heads, HEAD_DIM)
        v = v.reshape(m, n_heads, HEAD_DIM)
        for a, pieces in ((k, outs[0]), (v, outs[1])):
            for b in range(n_b):
                pieces += [a[r_t:r_t + N_META], a[b * seq:(b + 1) * seq]]
        outs[2].append(jnp.stack([u[(b + 1) * seq - POOL_HIST:(b + 1) * seq] for b in range(n_b)]))
        outs[3].append(k[r_p:r_t].reshape(s_b, s_len, n_heads, HEAD_DIM))
        outs[4].append(v[r_p:r_t].reshape(s_b, s_len, n_heads, HEAD_DIM))
        outs[5].append(tails[seq_of(r_p):seq_of(r_t), HALO - POOL_HIST:])

    y_prompt = x[:r_p].reshape(n_b, seq, d)
    y_sample = x[r_p:r_t].reshape(s_b, s_len, d)
    prompt_cache = lambda pieces: jnp.concatenate(pieces, axis=0).reshape(
        depth, n_b, N_META + seq, n_heads, HEAD_DIM)
    return (y_prompt, y_sample, prompt_cache(outs[0]), prompt_cache(outs[1])) + tuple(
        jnp.stack(o) for o in outs[2:])
```

```python
import functools

import jax
import jax.numpy as jnp
from jax import lax
from jax.experimental import pallas as pl
from jax.experimental.pallas import tpu as pltpu

F32 = jnp.float32
BF16 = jnp.bfloat16

RMS_EPS = 1e-6
LOG2_E = 1.4426950408889634
HEAD_DIM = 128
MXU_WIDTH = 256
N_META = 16
POOL_WINDOWS = (2, 4, 8, 16)
POOL_HIST = max(POOL_WINDOWS) - 1
HALO = 16
SEQ_TILE = 64
ROW_BLOCK = 128
LOOP_BLOCKS = 2
V7X_VMEM_BYTES = 64 * 1024 * 1024
V7X_VMEM_LIMIT = V7X_VMEM_BYTES - 4 * 1024 * 1024


def _rms(x, g):
    ms = jnp.mean(x * x, axis=-1, keepdims=True)
    return x * lax.rsqrt(ms + RMS_EPS) * g


def _dot(a, b):
    return jnp.dot(a, b, preferred_element_type=F32)


def _ffn_kernel(x_hbm, gi_ref, go_ref, wg_ref, wu_ref, wd_ref, *rest, n_f, out_rows):
    o_hbms = rest[:len(out_rows) - 1]
    x_ref, o_ref, xn_ref, sem_in, sem_out = rest[len(out_rows) - 1:]
    i = pl.program_id(0)
    j = pl.program_id(1)
    tm = x_ref.shape[0]
    n_rb = tm // ROW_BLOCK

    def rows_of(r):
        return pl.ds(r * ROW_BLOCK, ROW_BLOCK)

    def x_copy(r):
        return pltpu.make_async_copy(x_hbm.at[pl.ds(i * tm + r * ROW_BLOCK, ROW_BLOCK), :],
                                     x_ref.at[rows_of(r), :], sem_in.at[r])

    def for_o_copy(tile, r, action):
        first = tile * tm + r * ROW_BLOCK
        for o_hbm, lo, hi in zip(o_hbms, out_rows[:-1], out_rows[1:]):
            @pl.when((first >= lo) & (first < hi))
            def _():
                action(pltpu.make_async_copy(o_ref.at[rows_of(r), :],
                                             o_hbm.at[pl.ds(first - lo, ROW_BLOCK), :], sem_out.at[r]))

    @pl.when(j == 0)
    def _():
        for r in range(n_rb):
            x_copy(r).start()
        gi = gi_ref[...]
        for r in range(n_rb):
            x_copy(r).wait()

            @pl.when(i > 0)
            def _():
                for_o_copy(i - 1, r, lambda c: c.wait())

            xn_ref[rows_of(r), :] = _rms(x_ref[rows_of(r), :], gi).astype(BF16)
            o_ref[rows_of(r), :] = jnp.zeros((ROW_BLOCK, o_ref.shape[1]), F32)

    xn = xn_ref[...]
    tf = wg_ref.shape[1]
    hidden = []
    for c in range(0, tf, MXU_WIDTH):
        g = _dot(xn, wg_ref[:, c:c + MXU_WIDTH].astype(BF16))
        u = _dot(xn, wu_ref[:, c:c + MXU_WIDTH].astype(BF16))
        hidden.append((g * jax.nn.sigmoid(g) * u).astype(BF16))
    hidden = jnp.concatenate(hidden, axis=1)
    for c in range(0, o_ref.shape[1], MXU_WIDTH):
        o_ref[:, c:c + MXU_WIDTH] += _dot(hidden, wd_ref[:, c:c + MXU_WIDTH].astype(BF16))

    @pl.when(j == n_f - 1)
    def _():
        go = go_ref[...]
        for r in range(n_rb):
            o_ref[rows_of(r), :] = x_ref[rows_of(r), :] + 0.5 * _rms(o_ref[rows_of(r), :], go)
            for_o_copy(i, r, lambda c: c.start())

        @pl.when(i == pl.num_programs(0) - 1)
        def _():
            for r in range(n_rb):
                for_o_copy(i, r, lambda c: c.wait())


def _ffn(x, gi, go, wg, wu, wd, *, layer, tm, tf, out_rows=None):
    m, d = x.shape
    f = wg.shape[2]
    n_f = f // tf
    n_rb = tm // ROW_BLOCK
    out_rows = (0, m) if out_rows is None else tuple(out_rows)
    assert all(b % ROW_BLOCK == 0 for b in out_rows)
    n_out = len(out_rows) - 1
    outs = pl.pallas_call(
        functools.partial(_ffn_kernel, n_f=n_f, out_rows=out_rows),
        grid=(m // tm, n_f),
        in_specs=[
            pl.BlockSpec(memory_space=pl.ANY),
            pl.BlockSpec((1, d), lambda i, j: (0, 0)),
            pl.BlockSpec((1, d), lambda i, j: (0, 0)),
            pl.BlockSpec((None, d, tf), lambda i, j: (layer, 0, j)),
            pl.BlockSpec((None, d, tf), lambda i, j: (layer, 0, j)),
            pl.BlockSpec((None, tf, d), lambda i, j: (layer, j, 0)),
        ],
        out_specs=[pl.BlockSpec(memory_space=pl.ANY)] * n_out,
        out_shape=[jax.ShapeDtypeStruct((hi - lo, d), F32) for lo, hi in zip(out_rows[:-1], out_rows[1:])],
        scratch_shapes=[pltpu.VMEM((tm, d), F32), pltpu.VMEM((tm, d), F32), pltpu.VMEM((tm, d), BF16),
                        pltpu.SemaphoreType.DMA((n_rb,)), pltpu.SemaphoreType.DMA((n_rb,))],
        compiler_params=pltpu.CompilerParams(
            dimension_semantics=("arbitrary", "arbitrary"), vmem_limit_bytes=V7X_VMEM_LIMIT),
        name="ffn",
    )(x, gi, go, wg, wu, wd)
    return outs[0] if n_out == 1 else outs


def _inproj_kernel(x_ref, g_ref, w_ref, q_ref, kb_ref, vb_ref, k_ref, v_ref, u_ref, *, q_scale, n_heads):
    tm = x_ref.shape[0]
    width = q_ref.shape[1]
    xn = _rms(x_ref[...], g_ref[...]).astype(BF16)

    def proj(c):
        return _dot(xn, w_ref[:, c * width:(c + 1) * width])

    q_ref[...] = (proj(0) * q_scale).astype(BF16)
    for c, ref, bf_ref in ((1, k_ref, kb_ref), (2, v_ref, vb_ref)):
        p = proj(c)
        for h in range(n_heads):
            ref[pl.ds(h, tm, stride=n_heads), :] = p[:, _lanes(h)]
        bf_ref[...] = p.astype(BF16)
    u_ref[...] = proj(3)


def _inproj(x, g, w_in, *, layer, tm, width):
    m, d = x.shape
    n_heads = width // HEAD_DIM
    row_blk = lambda i: (i, 0)
    wide = pl.BlockSpec((tm, width), row_blk)
    by_head = pl.BlockSpec((tm * n_heads, HEAD_DIM), row_blk)
    out_bf = jax.ShapeDtypeStruct((m, width), BF16)
    out_heads = jax.ShapeDtypeStruct((m * n_heads, HEAD_DIM), F32)
    return pl.pallas_call(
        functools.partial(_inproj_kernel, q_scale=HEAD_DIM ** -0.5, n_heads=n_heads),
        grid=(m // tm,),
        in_specs=[
            pl.BlockSpec((tm, d), row_blk),
            pl.BlockSpec((1, d), lambda i: (0, 0)),
            pl.BlockSpec((None,) + w_in.shape[1:], lambda i: (layer, 0, 0), pipeline_mode=pl.Buffered(1)),
        ],
        out_specs=[wide, wide, wide, by_head, by_head, wide],
        out_shape=[out_bf, out_bf, out_bf, out_heads, out_heads, jax.ShapeDtypeStruct((m, width), F32)],
        compiler_params=pltpu.CompilerParams(
            dimension_semantics=("arbitrary",), vmem_limit_bytes=V7X_VMEM_LIMIT),
        name="inproj",
    )(x, g, w_in)


def _lanes(h):
    return slice(h * HEAD_DIM, (h + 1) * HEAD_DIM)


def _sb_block(qs, ks, vs, tri, mask):
    return _sb_blocks(qs, [(ks, vs, mask)], tri)[0]


def _sb_blocks(qs, blocks, tri):
    tq = qs[0].shape[0]
    zs = [jnp.concatenate(
        [lax.dot_general(q, k, (((1,), (1,)), ((), ())), preferred_element_type=F32) for q, k in zip(qs, ks)],
        axis=0) for ks, _, _ in blocks]
    suffixes = []
    for z, (_, _, mask) in zip(zs, blocks):
        sp = jnp.maximum(z, 0.0) + jnp.log(1.0 + jnp.exp2(jnp.abs(z) * -LOG2_E))
        if mask is not None:
            sp = jnp.where(mask, sp, 0.0)
        tk = z.shape[1]
        suffixes.append(_dot(sp.astype(BF16), tri[:tk, :tk]))
    out = []
    for z, suffix, (_, vs, mask) in zip(zs, suffixes, blocks):
        arg = z - suffix
        if mask is not None:
            arg = jnp.where(mask, arg, -jnp.inf)
        pb = jnp.exp(arg.astype(BF16))
        pv = jnp.concatenate([_dot(pb[h * tq:(h + 1) * tq], v) for h, v in enumerate(vs)], axis=0)
        out.append((pv, suffix[:, :1]))
    return out


def _sb_accumulate(carry_ref, acc_ref, blocks):
    neg_carry = carry_ref[...]
    acc = acc_ref[...]
    for pv, row_sum in blocks:
        acc = acc + jnp.exp(neg_carry) * pv
        neg_carry = neg_carry - row_sum
    carry_ref[...] = neg_carry
    acc_ref[...] = acc


def _sb_step(qs, ks, vs, carry_ref, acc_ref, tri, mask):
    _sb_accumulate(carry_ref, acc_ref, [_sb_block(qs, ks, vs, tri, mask)])


def _attn_prompt_kernel(q_ref, k_ref, v_ref, mk_ref, mv_ref, tri_ref, _o_in_ref, o_ref, carry_ref, acc_ref,
                        meta_pv_ref, meta_sum_ref, *, t, hg):
    i = pl.program_id(2)
    tri = tri_ref[...]
    heads = range(hg)
    qs = [q_ref[:, _lanes(h)] for h in heads]
    carry_ref[...] = jnp.zeros_like(carry_ref)
    acc_ref[...] = jnp.zeros_like(acc_ref)

    def token_block(c, mask=None):
        off = pl.multiple_of(c * t, t)
        return ([k_ref[pl.ds(off, t), _lanes(h)] for h in heads],
                [v_ref[pl.ds(off, t), _lanes(h)] for h in heads], mask)

    row = lax.broadcasted_iota(jnp.int32, (hg * t, t), 0) & (t - 1)
    col = lax.broadcasted_iota(jnp.int32, (hg * t, t), 1)
    mcol = lax.broadcasted_iota(jnp.int32, (hg * t, mk_ref.shape[0]), 1)
    diag = token_block(i, col < row)
    meta = ([mk_ref[:, _lanes(h)] for h in heads], [mv_ref[:, _lanes(h)] for h in heads], mcol < N_META)

    def head_group(token_blocks):
        *done, (meta_pv, meta_sum) = _sb_blocks(qs, token_blocks + [meta], tri)
        _sb_accumulate(carry_ref, acc_ref, done)
        meta_pv_ref[...] = meta_pv
        meta_sum_ref[...] = meta_sum

    for extra in range(LOOP_BLOCKS):
        @pl.when(i % LOOP_BLOCKS == extra)
        def _():
            head_group([diag] + [token_block(i - 1 - n) for n in range(extra)])

    top = i - i % LOOP_BLOCKS

    def body(s, _):
        first = top - 1 - LOOP_BLOCKS * s
        _sb_accumulate(carry_ref, acc_ref,
                       _sb_blocks(qs, [token_block(first - n) for n in range(LOOP_BLOCKS)], tri))
        return 0

    lax.fori_loop(0, top // LOOP_BLOCKS, body, 0)
    _sb_accumulate(carry_ref, acc_ref, [(meta_pv_ref[...], meta_sum_ref[...])])
    for h in heads:
        o_ref[:, _lanes(h)] = acc_ref[h * t:(h + 1) * t, :].astype(o_ref.dtype)


def _attn_prompt(q, kb, vb, tri, o_all, *, n_batch, seq, n_heads, meta_blk, hg):
    t = tri.shape[0]
    assert t & (t - 1) == 0 and seq % t == 0 and n_heads % hg == 0
    n_q = seq // t
    w = hg * HEAD_DIM
    return pl.pallas_call(
        functools.partial(_attn_prompt_kernel, t=t, hg=hg),
        grid=(n_batch, n_heads // hg, n_q),
        in_specs=[
            pl.BlockSpec((t, w), lambda b, g, i: (b * n_q + i, g)),
            pl.BlockSpec((seq, w), lambda b, g, i: (b, g)),
            pl.BlockSpec((seq, w), lambda b, g, i: (b, g)),
            pl.BlockSpec((ROW_BLOCK, w), lambda b, g, i: (meta_blk, g)),
            pl.BlockSpec((ROW_BLOCK, w), lambda b, g, i: (meta_blk, g)),
            pl.BlockSpec(tri.shape, lambda b, g, i: (0, 0)),
            pl.BlockSpec(memory_space=pl.ANY),
        ],
        out_specs=pl.BlockSpec((t, w), lambda b, g, i: (b * n_q + i, g)),
        out_shape=jax.ShapeDtypeStruct(o_all.shape, o_all.dtype),
        input_output_aliases={6: 0},
        scratch_shapes=[pltpu.VMEM((hg * t, 1), F32), pltpu.VMEM((hg * t, HEAD_DIM), F32),
                        pltpu.VMEM((hg * t, HEAD_DIM), F32), pltpu.VMEM((hg * t, 1), F32)],
        compiler_params=pltpu.CompilerParams(
            dimension_semantics=("arbitrary", "arbitrary", "arbitrary"), vmem_limit_bytes=V7X_VMEM_LIMIT),
        name="attn_prompt",
    )(q, kb, vb, kb, vb, tri, o_all)


def _attn_meta_kernel(q_ref, k_ref, v_ref, tri_ref, _o_in_ref, o_ref, carry_ref, acc_ref, *, n_heads):
    n = q_ref.shape[0]
    carry_ref[...] = jnp.zeros_like(carry_ref)
    acc_ref[...] = jnp.zeros_like(acc_ref)
    row = lax.broadcasted_iota(jnp.int32, (n_heads * n, n), 0) & (n - 1)
    col = lax.broadcasted_iota(jnp.int32, (n_heads * n, n), 1)
    mask = (col < row) & (col < N_META)
    heads = range(n_heads)
    _sb_step([q_ref[:, _lanes(h)] for h in heads], [k_ref[:, _lanes(h)] for h in heads],
             [v_ref[:, _lanes(h)] for h in heads], carry_ref, acc_ref, tri_ref[...], mask)
    for h in heads:
        o_ref[:, _lanes(h)] = acc_ref[h * n:(h + 1) * n, :].astype(o_ref.dtype)


def _attn_meta(q, kb, vb, tri, o_all, *, n_heads, meta_blk):
    w = n_heads * HEAD_DIM
    blk = pl.BlockSpec((ROW_BLOCK, w), lambda i: (meta_blk, 0))
    return pl.pallas_call(
        functools.partial(_attn_meta_kernel, n_heads=n_heads),
        grid=(1,),
        in_specs=[blk, blk, blk, pl.BlockSpec(tri.shape, lambda i: (0, 0)), pl.BlockSpec(memory_space=pl.ANY)],
        out_specs=blk,
        out_shape=jax.ShapeDtypeStruct(o_all.shape, o_all.dtype),
        input_output_aliases={4: 0},
        scratch_shapes=[pltpu.VMEM((n_heads * ROW_BLOCK, 1), F32), pltpu.VMEM((n_heads * ROW_BLOCK, HEAD_DIM), F32)],
        compiler_params=pltpu.CompilerParams(dimension_semantics=("arbitrary",)),
        name="attn_meta",
    )(q, kb, vb, tri, o_all)


def _attn_sample_kernel(q_ref, nk_ref, nv_ref, ck_ref, cv_ref, tri_ref, _o_in_ref, o_ref, carry_ref, acc_ref,
                        *, n_heads, tk, n_chunks):
    t = pl.program_id(1)
    lq = q_ref.shape[0]
    heads = range(n_heads)
    tri = tri_ref[...]
    qs = [q_ref[:, _lanes(h)] for h in heads]

    @pl.when(t == 0)
    def _():
        carry_ref[...] = jnp.zeros_like(carry_ref)
        acc_ref[...] = jnp.zeros_like(acc_ref)
        row = lax.broadcasted_iota(jnp.int32, (n_heads * lq, lq), 0) & (lq - 1)
        col = lax.broadcasted_iota(jnp.int32, (n_heads * lq, lq), 1)
        _sb_step(qs, [nk_ref[:, _lanes(h)] for h in heads], [nv_ref[:, _lanes(h)] for h in heads],
                 carry_ref, acc_ref, tri, col < row)

    n_sub = ck_ref.shape[0] // (tk * n_heads)

    def sub_block(s):
        head_rows = lambda h: pl.ds(s * tk * n_heads + h, tk, stride=n_heads)
        return ([ck_ref[head_rows(h), :].astype(BF16) for h in heads],
                [cv_ref[head_rows(h), :].astype(BF16) for h in heads], None)

    order = list(reversed(range(n_sub)))
    results = []
    for g in range(0, n_sub, LOOP_BLOCKS):
        results += _sb_blocks(qs, [sub_block(s) for s in order[g:g + LOOP_BLOCKS]], tri)
    _sb_accumulate(carry_ref, acc_ref, results)

    @pl.when(t == n_chunks - 1)
    def _():
        for h in heads:
            o_ref[:, _lanes(h)] = acc_ref[h * lq:(h + 1) * lq, :].astype(o_ref.dtype)


def _attn_sample(q, kb, vb, cache_k, cache_v, tri, o_all, *, layer, lq, row0, chunk):
    _, n_batch, past, n_heads, _ = cache_k.shape
    assert lq & (lq - 1) == 0
    tk = tri.shape[0]
    width = n_heads * HEAD_DIM
    n_chunks = past // chunk
    blk0 = row0 // lq
    new_blk = pl.BlockSpec((lq, width), lambda b, t: (blk0 + b, 0))
    cache_rows = lambda c: c.reshape(-1, HEAD_DIM)
    cache_blk = pl.BlockSpec((chunk * n_heads, HEAD_DIM),
                             lambda b, t: ((layer * n_batch + b) * n_chunks + n_chunks - 1 - t, 0))
    return pl.pallas_call(
        functools.partial(_attn_sample_kernel, n_heads=n_heads, tk=tk, n_chunks=n_chunks),
        grid=(n_batch, n_chunks),
        in_specs=[new_blk, new_blk, new_blk, cache_blk, cache_blk, pl.BlockSpec(tri.shape, lambda b, t: (0, 0)),
                  pl.BlockSpec(memory_space=pl.ANY)],
        out_specs=new_blk,
        out_shape=jax.ShapeDtypeStruct(o_all.shape, o_all.dtype),
        input_output_aliases={6: 0},
        scratch_shapes=[pltpu.VMEM((n_heads * lq, 1), F32), pltpu.VMEM((n_heads * lq, HEAD_DIM), F32)],
        compiler_params=pltpu.CompilerParams(
            dimension_semantics=("arbitrary", "arbitrary"), vmem_limit_bytes=V7X_VMEM_LIMIT),
        name="attn_sample",
    )(q, kb, vb, cache_rows(cache_k), cache_rows(cache_v), tri, o_all)


def _mixout_kernel(osb_ref, u_ref, halo_ref, h_ref, wp_ref, ps_ref, wo_ref, g_ref, o_ref, ext_ref, cat_ref,
                   *, meta_seq, sb_width):
    n_seq, lt, pw = u_ref.shape
    cg = pw // len(POOL_WINDOWS)
    rows = n_seq * lt
    ext_ref[:, :HALO, :] = halo_ref[...]
    ext_ref[:, HALO:, :] = u_ref[...]
    cat_ref[:, :sb_width] = osb_ref[...]

    seq_id = pl.program_id(0) * n_seq + lax.broadcasted_iota(jnp.int32, (n_seq, 1, 1), 0)
    n_hist = jnp.where(seq_id >= meta_seq, 0, POOL_HIST)
    n_before = (n_hist + lax.broadcasted_iota(jnp.int32, (n_seq, lt, 1), 1)).astype(F32)

    for g, w in enumerate(POOL_WINDOWS):
        lanes = pl.ds(g * cg, cg)
        cur = ext_ref[:, pl.ds(HALO, lt), lanes]
        s = cur
        for back in range(1, w):
            s = s + ext_ref[:, pl.ds(HALO - back, lt), lanes]
        cnt = jnp.minimum(float(w), n_before + 1.0)
        dlt = (s / cnt - cur).reshape(rows, cg).astype(BF16)
        y = _dot(dlt, wp_ref[g]) * ps_ref[:, lanes]
        cat_ref[:, pl.ds(sb_width + g * cg, cg)] = y.astype(BF16)

    mixed = _dot(cat_ref[...], wo_ref[...])
    o_ref[...] = h_ref[...] + _rms(mixed, g_ref[...])


def _mixout(o_sb, u, halo, h, w_pool, pool_scale, w_out, g, *, layer, tm):
    m, d = h.shape
    pw = u.shape[1]
    sb_width = o_sb.shape[1]
    seq_per = tm // SEQ_TILE
    u3 = u.reshape(m // SEQ_TILE, SEQ_TILE, pw)
    return pl.pallas_call(
        functools.partial(_mixout_kernel, meta_seq=(m - ROW_BLOCK) // SEQ_TILE, sb_width=sb_width),
        grid=(m // tm,),
        in_specs=[
            pl.BlockSpec((tm, sb_width), lambda i: (i, 0)),
            pl.BlockSpec((seq_per, SEQ_TILE, pw), lambda i: (i, 0, 0)),
            pl.BlockSpec((seq_per, HALO, pw), lambda i: (i, 0, 0)),
            pl.BlockSpec((tm, d), lambda i: (i, 0)),
            pl.BlockSpec((None,) + w_pool.shape[1:], lambda i: (layer, 0, 0, 0)),
            pl.BlockSpec((1, pw), lambda i: (0, 0)),
            pl.BlockSpec((None,) + w_out.shape[1:], lambda i: (layer, 0, 0)),
            pl.BlockSpec((1, d), lambda i: (0, 0)),
        ],
        out_specs=pl.BlockSpec((tm, d), lambda i: (i, 0)),
        out_shape=jax.ShapeDtypeStruct((m, d), F32),
        scratch_shapes=[pltpu.VMEM((seq_per, HALO + SEQ_TILE, pw), F32),
                        pltpu.VMEM((tm, sb_width + pw), BF16)],
        compiler_params=pltpu.CompilerParams(
            dimension_semantics=("arbitrary",), vmem_limit_bytes=V7X_VMEM_LIMIT),
        name="mixout",
    )(o_sb, u3, halo, h, w_pool, pool_scale, w_out, g)


def _row_tile(m, cap):
    blocks = m // ROW_BLOCK
    best = 1
    for dvs in range(1, blocks + 1):
        if blocks % dvs == 0 and dvs * ROW_BLOCK <= cap:
            best = dvs
    return best * ROW_BLOCK


def _col_tile(n, cap):
    best = HEAD_DIM
    for t in range(HEAD_DIM, cap + 1, HEAD_DIM):
        if n % t == 0:
            best = t
    return best


def kernel(x_prompt, x_sample, cache_k, cache_v, state_pool, meta_tokens, w_in, w_out, w_pool, pool_scale,
           norm_gains, ffn1_gate, ffn1_up, ffn1_down, ffn2_gate, ffn2_up, ffn2_down):
    n_b, seq, d = x_prompt.shape
    s_b, s_len, _ = x_sample.shape
    depth, _, past, n_heads, head_dim = cache_k.shape
    pw = state_pool.shape[-1]
    sbw = n_heads * head_dim
    assert head_dim == HEAD_DIM and meta_tokens.shape[0] == N_META and sbw == pw
    assert w_in.shape[-1] == 3 * sbw + pw and state_pool.shape[2] == POOL_HIST
    assert seq % SEQ_TILE == 0 and s_len == SEQ_TILE and seq >= POOL_HIST and N_META == HALO

    r_p = n_b * seq
    r_t = r_p + s_b * s_len
    assert r_t % ROW_BLOCK == 0
    m = r_t + ROW_BLOCK
    meta_blk = r_t // ROW_BLOCK

    tk = 256
    assert seq % tk == 0 and past % tk == 0
    chunk = next(c for c in (2048, 1024, tk) if past % c == 0)
    hg = next(g for g in (8, 4, 2, 1) if n_heads % g == 0)
    tri = (lax.broadcasted_iota(jnp.int32, (tk, tk), 0) >= lax.broadcasted_iota(jnp.int32, (tk, tk), 1)).astype(BF16)

    x = jnp.concatenate([
        x_prompt.reshape(r_p, d), x_sample.reshape(s_b * s_len, d), meta_tokens.astype(F32),
        jnp.zeros((ROW_BLOCK - N_META, d), F32)], axis=0)

    tm_ffn = _row_tile(m, 1152)
    tf = _col_tile(ffn1_gate.shape[-1], 512)
    tm_proj = _row_tile(m, 576)
    n_tiles_p = seq // SEQ_TILE

    w_in_bf, w_out_bf, w_pool_bf = (w.astype(BF16) for w in (w_in, w_out, w_pool))
    seq_of = lambda row: row // SEQ_TILE
    outs = [[] for _ in range(6)]
    for l in range(depth):
        gains = norm_gains[l].astype(F32)
        gain = lambda n: gains[n][None, :]

        x = _ffn(x, gain(0), gain(1), ffn1_gate, ffn1_up, ffn1_down, layer=l, tm=tm_ffn, tf=tf)
        q, kb, vb, k, v, u = _inproj(x, gain(2), w_in_bf, layer=l, tm=tm_proj, width=sbw)

        o_sb = jnp.zeros((m, sbw), BF16)
        o_sb = _attn_prompt(q, kb, vb, tri, o_sb, n_batch=n_b, seq=seq, n_heads=n_heads, meta_blk=meta_blk, hg=hg)
        o_sb = _attn_sample(q, kb, vb, cache_k, cache_v, tri, o_sb, layer=l, lq=s_len, row0=r_p, chunk=chunk)
        o_sb = _attn_meta(q, kb, vb, tri, o_sb, n_heads=n_heads, meta_blk=meta_blk)

        tails = u.reshape(seq_of(m), SEQ_TILE, pw)[:, SEQ_TILE - HALO:]
        halo_p = jnp.concatenate(
            [part for b in range(n_b)
             for part in (u[r_t:r_t + N_META][None], tails[b * n_tiles_p:(b + 1) * n_tiles_p - 1])], axis=0)
        halo_s = jnp.pad(state_pool[l].astype(F32), ((0, 0), (HALO - POOL_HIST, 0), (0, 0)))
        halo = jnp.concatenate([halo_p, halo_s, jnp.zeros((ROW_BLOCK // SEQ_TILE, HALO, pw), F32)], axis=0)

        x = _mixout(o_sb, u, halo, x, w_pool_bf, pool_scale[l].astype(F32)[None, :], w_out_bf, gain(3),
                    layer=l, tm=tm_proj)
        x = _ffn(x, gain(4), gain(5), ffn2_gate, ffn2_up, ffn2_down, layer=l, tm=tm_ffn, tf=tf,
                 out_rows=(0, r_p, r_t) if l == depth - 1 else None)

        k = k.reshape(m, n_heads, HEAD_DIM)
        v = v.reshape(m, n_heads, HEAD_DIM)
        for a, pieces in ((k, outs[0]), (v, outs[1])):
            for b in range(n_b):
                pieces += [a[r_t:r_t + N_META], a[b * seq:(b + 1) * seq]]
        outs[2].append(jnp.stack([u[(b + 1) * seq - POOL_HIST:(b + 1) * seq] for b in range(n_b)]))
        outs[3].append(k[r_p:r_t].reshape(s_b, s_len, n_heads, HEAD_DIM))
        outs[4].append(v[r_p:r_t].reshape(s_b, s_len, n_heads, HEAD_DIM))
        outs[5].append(tails[seq_of(r_p):seq_of(r_t), HALO - POOL_HIST:])

    y_prompt = x[0].reshape(n_b, seq, d)
    y_sample = x[1].reshape(s_b, s_len, d)
    prompt_cache = lambda pieces: jnp.concatenate(pieces, axis=0).reshape(
        depth, n_b, N_META + seq, n_heads, HEAD_DIM)
    return (y_prompt, y_sample, prompt_cache(outs[0]), prompt_cache(outs[1])) + tuple(
        jnp.stack(o) for o in outs[2:])
```

```python
import functools

import jax
import jax.numpy as jnp
from jax import lax
from jax.experimental import pallas as pl
from jax.experimental.pallas import tpu as pltpu

F32 = jnp.float32
BF16 = jnp.bfloat16

RMS_EPS = 1e-6
LOG2_E = 1.4426950408889634
HEAD_DIM = 128
MXU_WIDTH = 256
N_META = 16
POOL_WINDOWS = (2, 4, 8, 16)
POOL_HIST = max(POOL_WINDOWS) - 1
HALO = 16
SEQ_TILE = 64
ROW_BLOCK = 128
LOOP_BLOCKS = 2
V7X_VMEM_BYTES = 64 * 1024 * 1024
V7X_VMEM_LIMIT = V7X_VMEM_BYTES - 4 * 1024 * 1024


def _rms(x, g):
    ms = jnp.mean(x * x, axis=-1, keepdims=True)
    return x * lax.rsqrt(ms + RMS_EPS) * g


def _dot(a, b):
    return jnp.dot(a, b, preferred_element_type=F32)


def _ffn_kernel(*refs, n_f, in_rows, out_rows):
    n_in, n_out = len(in_rows) - 1, len(out_rows) - 1
    x_hbms = refs[:n_in]
    gi_ref, go_ref, wg_ref, wu_ref, wd_ref = refs[n_in:n_in + 5]
    o_hbms = refs[n_in + 5:n_in + 5 + n_out]
    x_ref, o_ref, xn_ref, sem_in, sem_out = refs[n_in + 5 + n_out:]
    i = pl.program_id(0)
    j = pl.program_id(1)
    tm = x_ref.shape[0]
    n_rb = tm // ROW_BLOCK

    def rows_of(r):
        return pl.ds(r * ROW_BLOCK, ROW_BLOCK)

    def for_x_copy(r, action):
        first = i * tm + r * ROW_BLOCK
        for x_hbm, lo, hi in zip(x_hbms, in_rows[:-1], in_rows[1:]):
            @pl.when((first >= lo) & (first < hi))
            def _():
                action(pltpu.make_async_copy(x_hbm.at[pl.ds(first - lo, ROW_BLOCK), :],
                                             x_ref.at[rows_of(r), :], sem_in.at[r]))

    def for_o_copy(tile, r, action):
        first = tile * tm + r * ROW_BLOCK
        for o_hbm, lo, hi in zip(o_hbms, out_rows[:-1], out_rows[1:]):
            @pl.when((first >= lo) & (first < hi))
            def _():
                action(pltpu.make_async_copy(o_ref.at[rows_of(r), :],
                                             o_hbm.at[pl.ds(first - lo, ROW_BLOCK), :], sem_out.at[r]))

    @pl.when(j == 0)
    def _():
        for r in range(n_rb):
            for_x_copy(r, lambda c: c.start())
        gi = gi_ref[...]
        for r in range(n_rb):
            for_x_copy(r, lambda c: c.wait())

            @pl.when(i > 0)
            def _():
                for_o_copy(i - 1, r, lambda c: c.wait())

            xn_ref[rows_of(r), :] = _rms(x_ref[rows_of(r), :], gi).astype(BF16)
            o_ref[rows_of(r), :] = jnp.zeros((ROW_BLOCK, o_ref.shape[1]), F32)

    xn = xn_ref[...]
    tf = wg_ref.shape[1]
    hidden = []
    for c in range(0, tf, MXU_WIDTH):
        g = _dot(xn, wg_ref[:, c:c + MXU_WIDTH].astype(BF16))
        u = _dot(xn, wu_ref[:, c:c + MXU_WIDTH].astype(BF16))
        hidden.append((g * jax.nn.sigmoid(g) * u).astype(BF16))
    hidden = jnp.concatenate(hidden, axis=1)
    for c in range(0, o_ref.shape[1], MXU_WIDTH):
        o_ref[:, c:c + MXU_WIDTH] += _dot(hidden, wd_ref[:, c:c + MXU_WIDTH].astype(BF16))

    @pl.when(j == n_f - 1)
    def _():
        go = go_ref[...]
        for r in range(n_rb):
            o_ref[rows_of(r), :] = x_ref[rows_of(r), :] + 0.5 * _rms(o_ref[rows_of(r), :], go)
            for_o_copy(i, r, lambda c: c.start())

        @pl.when(i == pl.num_programs(0) - 1)
        def _():
            for r in range(n_rb):
                for_o_copy(i, r, lambda c: c.wait())


def _ffn(x, gi, go, wg, wu, wd, *, layer, tm, tf, out_rows=None):
    parts = list(x) if isinstance(x, (list, tuple)) else [x]
    in_rows = (0,)
    for part in parts:
        in_rows += (in_rows[-1] + part.shape[0],)
    m, d = in_rows[-1], parts[0].shape[1]
    f = wg.shape[2]
    n_f = f // tf
    n_rb = tm // ROW_BLOCK
    out_rows = (0, m) if out_rows is None else tuple(out_rows)
    assert all(b % ROW_BLOCK == 0 for b in in_rows + out_rows)
    n_out = len(out_rows) - 1
    outs = pl.pallas_call(
        functools.partial(_ffn_kernel, n_f=n_f, in_rows=in_rows, out_rows=out_rows),
        grid=(m // tm, n_f),
        in_specs=[pl.BlockSpec(memory_space=pl.ANY)] * len(parts) + [
            pl.BlockSpec((1, d), lambda i, j: (0, 0)),
            pl.BlockSpec((1, d), lambda i, j: (0, 0)),
            pl.BlockSpec((None, d, tf), lambda i, j: (layer, 0, j)),
            pl.BlockSpec((None, d, tf), lambda i, j: (layer, 0, j)),
            pl.BlockSpec((None, tf, d), lambda i, j: (layer, j, 0)),
        ],
        out_specs=[pl.BlockSpec(memory_space=pl.ANY)] * n_out,
        out_shape=[jax.ShapeDtypeStruct((hi - lo, d), F32) for lo, hi in zip(out_rows[:-1], out_rows[1:])],
        scratch_shapes=[pltpu.VMEM((tm, d), F32), pltpu.VMEM((tm, d), F32), pltpu.VMEM((tm, d), BF16),
                        pltpu.SemaphoreType.DMA((n_rb,)), pltpu.SemaphoreType.DMA((n_rb,))],
        compiler_params=pltpu.CompilerParams(
            dimension_semantics=("arbitrary", "arbitrary"), vmem_limit_bytes=V7X_VMEM_LIMIT),
        name="ffn",
    )(*parts, gi, go, wg, wu, wd)
    return outs[0] if n_out == 1 else outs


def _inproj_kernel(x_ref, g_ref, w_ref, q_ref, kb_ref, vb_ref, k_ref, v_ref, u_ref, *, q_scale, n_heads):
    tm = x_ref.shape[0]
    width = q_ref.shape[1]
    xn = _rms(x_ref[...], g_ref[...]).astype(BF16)

    def proj(c):
        return _dot(xn, w_ref[:, c * width:(c + 1) * width])

    q_ref[...] = (proj(0) * q_scale).astype(BF16)
    for c, ref, bf_ref in ((1, k_ref, kb_ref), (2, v_ref, vb_ref)):
        p = proj(c)
        for h in range(n_heads):
            ref[pl.ds(h, tm, stride=n_heads), :] = p[:, _lanes(h)]
        bf_ref[...] = p.astype(BF16)
    u_ref[...] = proj(3)


def _inproj(x, g, w_in, *, layer, tm, width):
    m, d = x.shape
    n_heads = width // HEAD_DIM
    row_blk = lambda i: (i, 0)
    wide = pl.BlockSpec((tm, width), row_blk)
    by_head = pl.BlockSpec((tm * n_heads, HEAD_DIM), row_blk)
    out_bf = jax.ShapeDtypeStruct((m, width), BF16)
    out_heads = jax.ShapeDtypeStruct((m * n_heads, HEAD_DIM), F32)
    return pl.pallas_call(
        functools.partial(_inproj_kernel, q_scale=HEAD_DIM ** -0.5, n_heads=n_heads),
        grid=(m // tm,),
        in_specs=[
            pl.BlockSpec((tm, d), row_blk),
            pl.BlockSpec((1, d), lambda i: (0, 0)),
            pl.BlockSpec((None,) + w_in.shape[1:], lambda i: (layer, 0, 0), pipeline_mode=pl.Buffered(1)),
        ],
        out_specs=[wide, wide, wide, by_head, by_head, wide],
        out_shape=[out_bf, out_bf, out_bf, out_heads, out_heads, jax.ShapeDtypeStruct((m, width), F32)],
        compiler_params=pltpu.CompilerParams(
            dimension_semantics=("arbitrary",), vmem_limit_bytes=V7X_VMEM_LIMIT),
        name="inproj",
    )(x, g, w_in)


def _lanes(h):
    return slice(h * HEAD_DIM, (h + 1) * HEAD_DIM)


def _sb_block(qs, ks, vs, tri, mask):
    return _sb_blocks(qs, [(ks, vs, mask)], tri)[0]


def _sb_blocks(qs, blocks, tri):
    tq = qs[0].shape[0]
    zs = [jnp.concatenate(
        [lax.dot_general(q, k, (((1,), (1,)), ((), ())), preferred_element_type=F32) for q, k in zip(qs, ks)],
        axis=0) for ks, _, _ in blocks]
    suffixes = []
    for z, (_, _, mask) in zip(zs, blocks):
        sp = jnp.maximum(z, 0.0) + jnp.log(1.0 + jnp.exp2(jnp.abs(z) * -LOG2_E))
        if mask is not None:
            sp = jnp.where(mask, sp, 0.0)
        tk = z.shape[1]
        suffixes.append(_dot(sp.astype(BF16), tri[:tk, :tk]))
    out = []
    for z, suffix, (_, vs, mask) in zip(zs, suffixes, blocks):
        arg = z - suffix
        if mask is not None:
            arg = jnp.where(mask, arg, -jnp.inf)
        pb = jnp.exp(arg.astype(BF16))
        pv = jnp.concatenate([_dot(pb[h * tq:(h + 1) * tq], v) for h, v in enumerate(vs)], axis=0)
        out.append((pv, suffix[:, :1]))
    return out


def _sb_accumulate(carry_ref, acc_ref, blocks):
    neg_carry = carry_ref[...]
    acc = acc_ref[...]
    for pv, row_sum in blocks:
        acc = acc + jnp.exp(neg_carry) * pv
        neg_carry = neg_carry - row_sum
    carry_ref[...] = neg_carry
    acc_ref[...] = acc


def _sb_step(qs, ks, vs, carry_ref, acc_ref, tri, mask):
    _sb_accumulate(carry_ref, acc_ref, [_sb_block(qs, ks, vs, tri, mask)])


def _attn_prompt_kernel(q_ref, k_ref, v_ref, mk_ref, mv_ref, tri_ref, _o_in_ref, o_ref, carry_ref, acc_ref,
                        meta_pv_ref, meta_sum_ref, *, t, hg):
    i = pl.program_id(2)
    tri = tri_ref[...]
    heads = range(hg)
    qs = [q_ref[:, _lanes(h)] for h in heads]
    carry_ref[...] = jnp.zeros_like(carry_ref)
    acc_ref[...] = jnp.zeros_like(acc_ref)

    def token_block(c, mask=None):
        off = pl.multiple_of(c * t, t)
        return ([k_ref[pl.ds(off, t), _lanes(h)] for h in heads],
                [v_ref[pl.ds(off, t), _lanes(h)] for h in heads], mask)

    row = lax.broadcasted_iota(jnp.int32, (hg * t, t), 0) & (t - 1)
    col = lax.broadcasted_iota(jnp.int32, (hg * t, t), 1)
    mcol = lax.broadcasted_iota(jnp.int32, (hg * t, mk_ref.shape[0]), 1)
    diag = token_block(i, col < row)
    meta = ([mk_ref[:, _lanes(h)] for h in heads], [mv_ref[:, _lanes(h)] for h in heads], mcol < N_META)

    def head_group(token_blocks):
        *done, (meta_pv, meta_sum) = _sb_blocks(qs, token_blocks + [meta], tri)
        _sb_accumulate(carry_ref, acc_ref, done)
        meta_pv_ref[...] = meta_pv
        meta_sum_ref[...] = meta_sum

    for extra in range(LOOP_BLOCKS):
        @pl.when(i % LOOP_BLOCKS == extra)
        def _():
            head_group([diag] + [token_block(i - 1 - n) for n in range(extra)])

    top = i - i % LOOP_BLOCKS

    def body(s, _):
        first = top - 1 - LOOP_BLOCKS * s
        _sb_accumulate(carry_ref, acc_ref,
                       _sb_blocks(qs, [token_block(first - n) for n in range(LOOP_BLOCKS)], tri))
        return 0

    lax.fori_loop(0, top // LOOP_BLOCKS, body, 0)
    _sb_accumulate(carry_ref, acc_ref, [(meta_pv_ref[...], meta_sum_ref[...])])
    for h in heads:
        o_ref[:, _lanes(h)] = acc_ref[h * t:(h + 1) * t, :].astype(o_ref.dtype)


def _attn_prompt(q, kb, vb, tri, o_all, *, n_batch, seq, n_heads, meta_blk, hg):
    t = tri.shape[0]
    assert t & (t - 1) == 0 and seq % t == 0 and n_heads % hg == 0
    n_q = seq // t
    w = hg * HEAD_DIM
    return pl.pallas_call(
        functools.partial(_attn_prompt_kernel, t=t, hg=hg),
        grid=(n_batch, n_heads // hg, n_q),
        in_specs=[
            pl.BlockSpec((t, w), lambda b, g, i: (b * n_q + i, g)),
            pl.BlockSpec((seq, w), lambda b, g, i: (b, g)),
            pl.BlockSpec((seq, w), lambda b, g, i: (b, g)),
            pl.BlockSpec((ROW_BLOCK, w), lambda b, g, i: (meta_blk, g)),
            pl.BlockSpec((ROW_BLOCK, w), lambda b, g, i: (meta_blk, g)),
            pl.BlockSpec(tri.shape, lambda b, g, i: (0, 0)),
            pl.BlockSpec(memory_space=pl.ANY),
        ],
        out_specs=pl.BlockSpec((t, w), lambda b, g, i: (b * n_q + i, g)),
        out_shape=jax.ShapeDtypeStruct(o_all.shape, o_all.dtype),
        input_output_aliases={6: 0},
        scratch_shapes=[pltpu.VMEM((hg * t, 1), F32), pltpu.VMEM((hg * t, HEAD_DIM), F32),
                        pltpu.VMEM((hg * t, HEAD_DIM), F32), pltpu.VMEM((hg * t, 1), F32)],
        compiler_params=pltpu.CompilerParams(
            dimension_semantics=("arbitrary", "arbitrary", "arbitrary"), vmem_limit_bytes=V7X_VMEM_LIMIT),
        name="attn_prompt",
    )(q, kb, vb, kb, vb, tri, o_all)


def _attn_meta_kernel(q_ref, k_ref, v_ref, tri_ref, _o_in_ref, o_ref, carry_ref, acc_ref, *, n_heads):
    n = q_ref.shape[0]
    carry_ref[...] = jnp.zeros_like(carry_ref)
    acc_ref[...] = jnp.zeros_like(acc_ref)
    row = lax.broadcasted_iota(jnp.int32, (n_heads * n, n), 0) & (n - 1)
    col = lax.broadcasted_iota(jnp.int32, (n_heads * n, n), 1)
    mask = (col < row) & (col < N_META)
    heads = range(n_heads)
    _sb_step([q_ref[:, _lanes(h)] for h in heads], [k_ref[:, _lanes(h)] for h in heads],
             [v_ref[:, _lanes(h)] for h in heads], carry_ref, acc_ref, tri_ref[...], mask)
    for h in heads:
        o_ref[:, _lanes(h)] = acc_ref[h * n:(h + 1) * n, :].astype(o_ref.dtype)


def _attn_meta(q, kb, vb, tri, o_all, *, n_heads, meta_blk):
    w = n_heads * HEAD_DIM
    blk = pl.BlockSpec((ROW_BLOCK, w), lambda i: (meta_blk, 0))
    return pl.pallas_call(
        functools.partial(_attn_meta_kernel, n_heads=n_heads),
        grid=(1,),
        in_specs=[blk, blk, blk, pl.BlockSpec(tri.shape, lambda i: (0, 0)), pl.BlockSpec(memory_space=pl.ANY)],
        out_specs=blk,
        out_shape=jax.ShapeDtypeStruct(o_all.shape, o_all.dtype),
        input_output_aliases={4: 0},
        scratch_shapes=[pltpu.VMEM((n_heads * ROW_BLOCK, 1), F32), pltpu.VMEM((n_heads * ROW_BLOCK, HEAD_DIM), F32)],
        compiler_params=pltpu.CompilerParams(dimension_semantics=("arbitrary",)),
        name="attn_meta",
    )(q, kb, vb, tri, o_all)


def _attn_sample_kernel(q_ref, nk_ref, nv_ref, ck_ref, cv_ref, tri_ref, _o_in_ref, o_ref, carry_ref, acc_ref,
                        *, n_heads, tk, n_chunks):
    t = pl.program_id(1)
    lq = q_ref.shape[0]
    heads = range(n_heads)
    tri = tri_ref[...]
    qs = [q_ref[:, _lanes(h)] for h in heads]

    @pl.when(t == 0)
    def _():
        carry_ref[...] = jnp.zeros_like(carry_ref)
        acc_ref[...] = jnp.zeros_like(acc_ref)
        row = lax.broadcasted_iota(jnp.int32, (n_heads * lq, lq), 0) & (lq - 1)
        col = lax.broadcasted_iota(jnp.int32, (n_heads * lq, lq), 1)
        _sb_step(qs, [nk_ref[:, _lanes(h)] for h in heads], [nv_ref[:, _lanes(h)] for h in heads],
                 carry_ref, acc_ref, tri, col < row)

    n_sub = ck_ref.shape[0] // (tk * n_heads)

    def sub_block(s):
        head_rows = lambda h: pl.ds(s * tk * n_heads + h, tk, stride=n_heads)
        return ([ck_ref[head_rows(h), :].astype(BF16) for h in heads],
                [cv_ref[head_rows(h), :].astype(BF16) for h in heads], None)

    order = list(reversed(range(n_sub)))
    results = []
    for g in range(0, n_sub, LOOP_BLOCKS):
        results += _sb_blocks(qs, [sub_block(s) for s in order[g:g + LOOP_BLOCKS]], tri)
    _sb_accumulate(carry_ref, acc_ref, results)

    @pl.when(t == n_chunks - 1)
    def _():
        for h in heads:
            o_ref[:, _lanes(h)] = acc_ref[h * lq:(h + 1) * lq, :].astype(o_ref.dtype)


def _attn_sample(q, kb, vb, cache_k, cache_v, tri, o_all, *, layer, lq, row0, chunk):
    _, n_batch, past, n_heads, _ = cache_k.shape
    assert lq & (lq - 1) == 0
    tk = tri.shape[0]
    width = n_heads * HEAD_DIM
    n_chunks = past // chunk
    blk0 = row0 // lq
    new_blk = pl.BlockSpec((lq, width), lambda b, t: (blk0 + b, 0))
    cache_rows = lambda c: c.reshape(-1, HEAD_DIM)
    cache_blk = pl.BlockSpec((chunk * n_heads, HEAD_DIM),
                             lambda b, t: ((layer * n_batch + b) * n_chunks + n_chunks - 1 - t, 0))
    return pl.pallas_call(
        functools.partial(_attn_sample_kernel, n_heads=n_heads, tk=tk, n_chunks=n_chunks),
        grid=(n_batch, n_chunks),
        in_specs=[new_blk, new_blk, new_blk, cache_blk, cache_blk, pl.BlockSpec(tri.shape, lambda b, t: (0, 0)),
                  pl.BlockSpec(memory_space=pl.ANY)],
        out_specs=new_blk,
        out_shape=jax.ShapeDtypeStruct(o_all.shape, o_all.dtype),
        input_output_aliases={6: 0},
        scratch_shapes=[pltpu.VMEM((n_heads * lq, 1), F32), pltpu.VMEM((n_heads * lq, HEAD_DIM), F32)],
        compiler_params=pltpu.CompilerParams(
            dimension_semantics=("arbitrary", "arbitrary"), vmem_limit_bytes=V7X_VMEM_LIMIT),
        name="attn_sample",
    )(q, kb, vb, cache_rows(cache_k), cache_rows(cache_v), tri, o_all)


def _mixout_kernel(osb_ref, u_ref, halo_ref, h_ref, wp_ref, ps_ref, wo_ref, g_ref, o_ref, ext_ref, cat_ref,
                   *, meta_seq, sb_width):
    n_seq, lt, pw = u_ref.shape
    cg = pw // len(POOL_WINDOWS)
    rows = n_seq * lt
    ext_ref[:, :HALO, :] = halo_ref[...]
    ext_ref[:, HALO:, :] = u_ref[...]
    cat_ref[:, :sb_width] = osb_ref[...]

    seq_id = pl.program_id(0) * n_seq + lax.broadcasted_iota(jnp.int32, (n_seq, 1, 1), 0)
    n_hist = jnp.where(seq_id >= meta_seq, 0, POOL_HIST)
    n_before = (n_hist + lax.broadcasted_iota(jnp.int32, (n_seq, lt, 1), 1)).astype(F32)

    for g, w in enumerate(POOL_WINDOWS):
        lanes = pl.ds(g * cg, cg)
        cur = ext_ref[:, pl.ds(HALO, lt), lanes]
        s = cur
        for back in range(1, w):
            s = s + ext_ref[:, pl.ds(HALO - back, lt), lanes]
        cnt = jnp.minimum(float(w), n_before + 1.0)
        dlt = (s / cnt - cur).reshape(rows, cg).astype(BF16)
        y = _dot(dlt, wp_ref[g]) * ps_ref[:, lanes]
        cat_ref[:, pl.ds(sb_width + g * cg, cg)] = y.astype(BF16)

    mixed = _dot(cat_ref[...], wo_ref[...])
    o_ref[...] = h_ref[...] + _rms(mixed, g_ref[...])


def _mixout(o_sb, u, halo, h, w_pool, pool_scale, w_out, g, *, layer, tm):
    m, d = h.shape
    pw = u.shape[1]
    sb_width = o_sb.shape[1]
    seq_per = tm // SEQ_TILE
    u3 = u.reshape(m // SEQ_TILE, SEQ_TILE, pw)
    return pl.pallas_call(
        functools.partial(_mixout_kernel, meta_seq=(m - ROW_BLOCK) // SEQ_TILE, sb_width=sb_width),
        grid=(m // tm,),
        in_specs=[
            pl.BlockSpec((tm, sb_width), lambda i: (i, 0)),
            pl.BlockSpec((seq_per, SEQ_TILE, pw), lambda i: (i, 0, 0)),
            pl.BlockSpec((seq_per, HALO, pw), lambda i: (i, 0, 0)),
            pl.BlockSpec((tm, d), lambda i: (i, 0)),
            pl.BlockSpec((None,) + w_pool.shape[1:], lambda i: (layer, 0, 0, 0)),
            pl.BlockSpec((1, pw), lambda i: (0, 0)),
            pl.BlockSpec((None,) + w_out.shape[1:], lambda i: (layer, 0, 0)),
            pl.BlockSpec((1, d), lambda i: (0, 0)),
        ],
        out_specs=pl.BlockSpec((tm, d), lambda i: (i, 0)),
        out_shape=jax.ShapeDtypeStruct((m, d), F32),
        scratch_shapes=[pltpu.VMEM((seq_per, HALO + SEQ_TILE, pw), F32),
                        pltpu.VMEM((tm, sb_width + pw), BF16)],
        compiler_params=pltpu.CompilerParams(
            dimension_semantics=("arbitrary",), vmem_limit_bytes=V7X_VMEM_LIMIT),
        name="mixout",
    )(o_sb, u3, halo, h, w_pool, pool_scale, w_out, g)


def _row_tile(m, cap):
    blocks = m // ROW_BLOCK
    best = 1
    for dvs in range(1, blocks + 1):
        if blocks % dvs == 0 and dvs * ROW_BLOCK <= cap:
            best = dvs
    return best * ROW_BLOCK


def _col_tile(n, cap):
    best = HEAD_DIM
    for t in range(HEAD_DIM, cap + 1, HEAD_DIM):
        if n % t == 0:
            best = t
    return best


def kernel(x_prompt, x_sample, cache_k, cache_v, state_pool, meta_tokens, w_in, w_out, w_pool, pool_scale,
           norm_gains, ffn1_gate, ffn1_up, ffn1_down, ffn2_gate, ffn2_up, ffn2_down):
    n_b, seq, d = x_prompt.shape
    s_b, s_len, _ = x_sample.shape
    depth, _, past, n_heads, head_dim = cache_k.shape
    pw = state_pool.shape[-1]
    sbw = n_heads * head_dim
    assert head_dim == HEAD_DIM and meta_tokens.shape[0] == N_META and sbw == pw
    assert w_in.shape[-1] == 3 * sbw + pw and state_pool.shape[2] == POOL_HIST
    assert seq % SEQ_TILE == 0 and s_len == SEQ_TILE and seq >= POOL_HIST and N_META == HALO

    r_p = n_b * seq
    r_t = r_p + s_b * s_len
    assert r_t % ROW_BLOCK == 0
    m = r_t + ROW_BLOCK
    meta_blk = r_t // ROW_BLOCK

    tk = 256
    assert seq % tk == 0 and past % tk == 0
    chunk = next(c for c in (2048, 1024, tk) if past % c == 0)
    hg = next(g for g in (8, 4, 2, 1) if n_heads % g == 0)
    tri = (lax.broadcasted_iota(jnp.int32, (tk, tk), 0) >= lax.broadcasted_iota(jnp.int32, (tk, tk), 1)).astype(BF16)

    x = [x_prompt.reshape(r_p, d), x_sample.reshape(s_b * s_len, d),
         jnp.pad(meta_tokens.astype(F32), ((0, ROW_BLOCK - N_META), (0, 0)))]

    tm_ffn = _row_tile(m, 1152)
    tf = _col_tile(ffn1_gate.shape[-1], 512)
    tm_proj = _row_tile(m, 576)
    n_tiles_p = seq // SEQ_TILE

    w_in_bf, w_out_bf, w_pool_bf = (w.astype(BF16) for w in (w_in, w_out, w_pool))
    seq_of = lambda row: row // SEQ_TILE
    outs = [[] for _ in range(6)]
    for l in range(depth):
        gains = norm_gains[l].astype(F32)
        gain = lambda n: gains[n][None, :]

        x = _ffn(x, gain(0), gain(1), ffn1_gate, ffn1_up, ffn1_down, layer=l, tm=tm_ffn, tf=tf)
        q, kb, vb, k, v, u = _inproj(x, gain(2), w_in_bf, layer=l, tm=tm_proj, width=sbw)

        o_sb = jnp.zeros((m, sbw), BF16)
        o_sb = _attn_prompt(q, kb, vb, tri, o_sb, n_batch=n_b, seq=seq, n_heads=n_heads, meta_blk=meta_blk, hg=hg)
        o_sb = _attn_sample(q, kb, vb, cache_k, cache_v, tri, o_sb, layer=l, lq=s_len, row0=r_p, chunk=chunk)
        o_sb = _attn_meta(q, kb, vb, tri, o_sb, n_heads=n_heads, meta_blk=meta_blk)

        tails = u.reshape(seq_of(m), SEQ_TILE, pw)[:, SEQ_TILE - HALO:]
        halo_p = jnp.concatenate(
            [part for b in range(n_b)
             for part in (u[r_t:r_t + N_META][None], tails[b * n_tiles_p:(b + 1) * n_tiles_p - 1])], axis=0)
        halo_s = jnp.pad(state_pool[l].astype(F32), ((0, 0), (HALO - POOL_HIST, 0), (0, 0)))
        halo = jnp.concatenate([halo_p, halo_s, jnp.zeros((ROW_BLOCK // SEQ_TILE, HALO, pw), F32)], axis=0)

        x = _mixout(o_sb, u, halo, x, w_pool_bf, pool_scale[l].astype(F32)[None, :], w_out_bf, gain(3),
                    layer=l, tm=tm_proj)
        x = _ffn(x, gain(4), gain(5), ffn2_gate, ffn2_up, ffn2_down, layer=l, tm=tm_ffn, tf=tf,
                 out_rows=(0, r_p, r_t) if l == depth - 1 else None)

        k = k.reshape(m, n_heads, HEAD_DIM)
        v = v.reshape(m, n_heads, HEAD_DIM)
        for a, pieces in ((k, outs[0]), (v, outs[1])):
            for b in range(n_b):
                pieces += [a[r_t:r_t + N_META], a[b * seq:(b + 1) * seq]]
        outs[2].append(jnp.stack([u[(b + 1) * seq - POOL_HIST:(b + 1) * seq] for b in range(n_b)]))
        outs[3].append(k[r_p:r_t].reshape(s_b, s_len, n_heads, HEAD_DIM))
        outs[4].append(v[r_p:r_t].reshape(s_b, s_len, n_heads, HEAD_DIM))
        outs[5].append(tails[seq_of(r_p):seq_of(r_t), HALO - POOL_HIST:])

    y_prompt = x[0].reshape(n_b, seq, d)
    y_sample = x[1].reshape(s_b, s_len, d)
    prompt_cache = lambda pieces: jnp.concatenate(pieces, axis=0).reshape(
        depth, n_b, N_META + seq, n_heads, HEAD_DIM)
    return (y_prompt, y_sample, prompt_cache(outs[0]), prompt_cache(outs[1])) + tuple(
        jnp.stack(o) for o in outs[2:])
```
